```python
import jax, jax.numpy as jnp
from jax import lax
import numpy as np


D_MODEL = 1024
BATCH = 4
SEQ = 8192
DEPTH = 2

N_SUB = 3
HALF_STEP = 0.5
D_FF = 2816
CONV_WIDTH = 4
EPS = 1e-6
LRU_WIDTH = D_MODEL
LRU_HEADS = 8
LRU_BLOCK = LRU_WIDTH // LRU_HEADS
LRU_C = 8.0
SSD_WIDTH = D_MODEL
SSD_HEADDIM = 64
SSD_HEADS = SSD_WIDTH // SSD_HEADDIM
SSD_GROUPS = 2
SSD_STATE = 128
SSD_CHUNK = 128
SSD_CONV_DIM = SSD_WIDTH + 2 * SSD_GROUPS * SSD_STATE
HYB_SPLITS = (LRU_WIDTH, 2 * LRU_WIDTH, 2 * LRU_WIDTH + SSD_WIDTH, 2 * LRU_WIDTH + SSD_WIDTH + SSD_CONV_DIM)
HYB_IN = HYB_SPLITS[-1] + SSD_HEADS
HYB_OUT = LRU_WIDTH + SSD_WIDTH
MLSTM_WIDTH = 2 * D_MODEL
MLSTM_HEADS = 4
MLSTM_HEADDIM = MLSTM_WIDTH // MLSTM_HEADS
MLSTM_QKV_BLOCK = 4
MLSTM_CHUNK = 64
N_EVEN = (DEPTH + 1) // 2
N_ODD = DEPTH // 2

kernel_name = 'hybrid_rglru_ssd_mlstm_macaron_adaln'


def rms_norm(x, g):
    xf = x.astype(jnp.float32)
    y = xf * lax.rsqrt(jnp.mean(xf * xf, axis=-1, keepdims=True) + EPS)
    return (y * g.astype(jnp.float32)).astype(x.dtype)


def head_layer_norm(h, g):
    mu = jnp.mean(h, axis=-1, keepdims=True)
    hc = h - mu
    var = jnp.mean(hc * hc, axis=-1, keepdims=True)
    return hc * lax.rsqrt(var + EPS) * g.astype(jnp.float32)


def causal_conv(x, w, b):
    width, ch = w.shape
    y = lax.conv_general_dilated(x, w[:, None, :].astype(x.dtype), window_strides=(1,),
                                 padding=((width - 1, 0),), dimension_numbers=('NWC', 'WIO', 'NWC'),
                                 feature_group_count=ch)
    return y + b


def block_diag_linear(x, w):
    nb, bi, bo = w.shape
    xb = x.reshape(x.shape[:-1] + (nb, bi))
    return jnp.einsum('...ni,nio->...no', xb, w).reshape(x.shape[:-1] + (nb * bo,))


def swiglu(h, w_gate, w_up, w_down):
    return (jax.nn.silu(h @ w_gate) * (h @ w_up)) @ w_down


def rg_lru(x, w_a, b_a, w_x, b_x, lam):
    xf = x.astype(jnp.float32)
    r = jax.nn.sigmoid(block_diag_linear(xf, w_a.astype(jnp.float32)) + b_a.astype(jnp.float32))
    i = jax.nn.sigmoid(block_diag_linear(xf, w_x.astype(jnp.float32)) + b_x.astype(jnp.float32))
    log_a = -LRU_C * r * jax.nn.softplus(-lam.astype(jnp.float32))
    a = jnp.exp(log_a)
    u = jnp.sqrt(-jnp.expm1(2.0 * log_a)) * (i * xf)

    def combine(left, right):
        a_l, u_l = left
        a_r, u_r = right
        return a_l * a_r, a_r * u_l + u_r

    _, h = lax.associative_scan(combine, (a, u), axis=1)
    return h


def segsum(x):
    t = x.shape[-1]
    cs = jnp.cumsum(x, axis=-1)
    diff = cs[..., :, None] - cs[..., None, :]
    mask = jnp.tril(jnp.ones((t, t), dtype=bool))
    return jnp.where(mask, diff, -jnp.inf)


def ssd_chunked(x, dt, a, bm, cm):
    bsz, seq, nh, hp = x.shape
    ng, ns = bm.shape[-2:]
    ne = nh // ng
    nc = seq // SSD_CHUNK
    ln = SSD_CHUNK
    xc = (x * dt[..., None]).reshape(bsz, nc, ln, ng, ne, hp)
    bc = bm.reshape(bsz, nc, ln, ng, ns)
    cc = cm.reshape(bsz, nc, ln, ng, ns)
    ac = (dt * a).reshape(bsz, nc, ln, ng, ne).transpose(0, 3, 4, 1, 2)
    acs = jnp.cumsum(ac, axis=-1)
    decay_in = jnp.exp(segsum(ac))
    cb = jnp.einsum('bclgn,bcsgn->bgcls', cc, bc)
    y_diag = jnp.einsum('bgecls,bcsgep->bclgep', cb[:, :, None] * decay_in, xc)
    decay_states = jnp.exp(acs[..., -1:] - acs).transpose(0, 3, 4, 1, 2)
    states = jnp.einsum('bclgn,bclgep->bcgepn', bc, xc * decay_states[..., None])
    chunk_tot = jnp.pad(acs[..., -1], ((0, 0), (0, 0), (0, 0), (1, 0)))
    decay_chunk = jnp.exp(segsum(chunk_tot))
    states = jnp.pad(states, ((0, 0), (1, 0), (0, 0), (0, 0), (0, 0), (0, 0)))
    prev_states = jnp.einsum('bgezj,bjgepn->bzgepn', decay_chunk, states)[:, :-1]
    decay_out = jnp.exp(acs).transpose(0, 3, 4, 1, 2)
    y_off = jnp.einsum('bclgn,bcgepn->bclgep', cc, prev_states) * decay_out[..., None]
    return (y_diag + y_off).reshape(bsz, seq, nh, hp)


def mlstm_chunkwise(q, k, v, i_pre, f_pre):
    bsz, nh, seq, dh = q.shape
    nc = seq // MLSTM_CHUNK
    ln = MLSTM_CHUNK
    k = k * (dh ** -0.5)
    log_f = jax.nn.log_sigmoid(f_pre)

    def to_chunks(t):
        return jnp.moveaxis(t.reshape((bsz, nh, nc, ln) + t.shape[3:]), 2, 0)

    causal = jnp.tril(jnp.ones((ln, ln), dtype=bool))

    def step(carry, inp):
        c_st, n_st, m_st = carry
        q_c, k_c, v_c, i_c, lf_c = inp
        bcum = jnp.cumsum(lf_c, axis=-1)
        log_d = jnp.where(causal, bcum[..., :, None] - bcum[..., None, :] + i_c[..., None, :], -jnp.inf)
        g = bcum + m_st[..., None]
        m = jnp.maximum(g, jnp.max(log_d, axis=-1))
        w_inter = jnp.exp(g - m)
        s_mat = jnp.einsum('bhtd,bhsd->bhts', q_c, k_c) * jnp.exp(log_d - m[..., None])
        num = w_inter[..., None] * jnp.einsum('bhtd,bhdv->bhtv', q_c, c_st) + jnp.einsum('bhts,bhsv->bhtv', s_mat, v_c)
        den = w_inter * jnp.einsum('bhtd,bhd->bht', q_c, n_st) + jnp.sum(s_mat, axis=-1)
        h_out = num / jnp.maximum(jnp.abs(den), jnp.exp(-m))[..., None]
        b_last = bcum[..., -1]
        log_w = b_last[..., None] - bcum + i_c
        m_new = jnp.maximum(b_last + m_st, jnp.max(log_w, axis=-1))
        w_s = jnp.exp(log_w - m_new[..., None])
        decay = jnp.exp(b_last + m_st - m_new)
        c_new = decay[..., None, None] * c_st + jnp.einsum('bhs,bhsd,bhsv->bhdv', w_s, k_c, v_c)
        n_new = decay[..., None] * n_st + jnp.einsum('bhs,bhsd->bhd', w_s, k_c)
        return (c_new, n_new, m_new), h_out

    init = (jnp.zeros((bsz, nh, dh, dh), jnp.float32), jnp.zeros((bsz, nh, dh), jnp.float32),
            jnp.zeros((bsz, nh), jnp.float32))
    _, hc = lax.scan(step, init, (to_chunks(q), to_chunks(k), to_chunks(v), to_chunks(i_pre), to_chunks(log_f)))
    return jnp.moveaxis(hc, 0, 2).reshape(bsz, nh, seq, dh)


def hybrid_mixer(h, w_in, w_out, lru_conv_w, lru_conv_b, lru_wa, lru_ba, lru_wx, lru_bx, lru_lambda,
                 ssd_conv_w, ssd_conv_b, ssd_dt_bias, ssd_a_log, ssd_d, ssd_norm_g):
    bsz, seq, _ = h.shape
    f32 = jnp.float32
    gate_lru, x_lru, z_ssd, xbc, dt_raw = jnp.split(h @ w_in, HYB_SPLITS, axis=-1)
    x_lru = causal_conv(x_lru, lru_conv_w, lru_conv_b)
    y_lru = rg_lru(x_lru, lru_wa, lru_ba, lru_wx, lru_bx, lru_lambda) * jax.nn.gelu(gate_lru.astype(f32))
    xbc = jax.nn.silu(causal_conv(xbc, ssd_conv_w, ssd_conv_b)).astype(f32)
    gn = SSD_GROUPS * SSD_STATE
    xs = xbc[..., :SSD_WIDTH].reshape(bsz, seq, SSD_HEADS, SSD_HEADDIM)
    bm = xbc[..., SSD_WIDTH:SSD_WIDTH + gn].reshape(bsz, seq, SSD_GROUPS, SSD_STATE)
    cm = xbc[..., SSD_WIDTH + gn:].reshape(bsz, seq, SSD_GROUPS, SSD_STATE)
    dt = jax.nn.softplus(dt_raw.astype(f32) + ssd_dt_bias.astype(f32))
    a = -jnp.exp(ssd_a_log.astype(f32))
    y = ssd_chunked(xs, dt, a, bm, cm) + ssd_d.astype(f32)[:, None] * xs
    y = y.reshape(bsz, seq, SSD_WIDTH) * jax.nn.silu(z_ssd.astype(f32))
    y = rms_norm(y.reshape(bsz, seq, SSD_GROUPS, SSD_WIDTH // SSD_GROUPS),
                 ssd_norm_g.reshape(SSD_GROUPS, SSD_WIDTH // SSD_GROUPS)).reshape(bsz, seq, SSD_WIDTH)
    y_cat = jnp.concatenate([y_lru, y], axis=-1).astype(h.dtype)
    return y_cat @ w_out


def mlstm_block(h, w_up, conv_w, conv_b, wq, wk, wv, w_gates, b_gates, norm_g, skip, w_down):
    bsz, seq, _ = h.shape
    f32 = jnp.float32
    xm, z = jnp.split(h @ w_up, 2, axis=-1)
    xc = jax.nn.silu(causal_conv(xm, conv_w, conv_b))
    q = block_diag_linear(xc, wq)
    k = block_diag_linear(xc, wk)
    v = block_diag_linear(xm, wv)
    gates = (q @ w_gates[:MLSTM_WIDTH] + k @ w_gates[MLSTM_WIDTH:2 * MLSTM_WIDTH]
             + v @ w_gates[2 * MLSTM_WIDTH:] + b_gates).astype(f32).transpose(0, 2, 1)
    i_pre = gates[:, :MLSTM_HEADS]
    f_pre = gates[:, MLSTM_HEADS:]

    def heads(t):
        return t.astype(f32).reshape(bsz, seq, MLSTM_HEADS, MLSTM_HEADDIM).transpose(0, 2, 1, 3)

    hh = mlstm_chunkwise(heads(q), heads(k), heads(v), i_pre, f_pre)
    hh = head_layer_norm(hh.transpose(0, 2, 1, 3), norm_g.reshape(MLSTM_HEADS, MLSTM_HEADDIM))
    hh = hh.reshape(bsz, seq, MLSTM_WIDTH) + skip.astype(f32) * xc.astype(f32)
    return (hh * jax.nn.silu(z.astype(f32))).astype(h.dtype) @ w_down


def setup_inputs(seed: int = 0) -> dict:
    key = jax.random.key(seed)
    ks = iter(jax.random.split(key, 64))
    f32 = jnp.float32

    def nrm(shape, fan_in, scale=1.0):
        return jax.random.normal(next(ks), shape, f32) * (scale * fan_in ** -0.5)

    def small(shape, s=0.02):
        return jax.random.normal(next(ks), shape, f32) * s

    def gain(shape):
        return 1.0 + small(shape, 0.05)

    x = jax.random.normal(next(ks), (BATCH, SEQ, D_MODEL), f32)
    c = jax.random.normal(next(ks), (BATCH, D_MODEL), f32)
    ada_w = nrm((DEPTH, D_MODEL, N_SUB * 3 * D_MODEL), D_MODEL, 0.5)
    ada_b = small((DEPTH, N_SUB * 3 * D_MODEL))
    norm_g = gain((DEPTH, N_SUB, D_MODEL))
    ffn_w_gate = nrm((DEPTH, 2, D_MODEL, D_FF), D_MODEL)
    ffn_w_up = nrm((DEPTH, 2, D_MODEL, D_FF), D_MODEL)
    ffn_w_down = nrm((DEPTH, 2, D_FF, D_MODEL), D_FF)
    hyb_w_in = nrm((N_EVEN, D_MODEL, HYB_IN), D_MODEL)
    hyb_w_out = nrm((N_EVEN, HYB_OUT, D_MODEL), HYB_OUT)
    lru_conv_w = nrm((N_EVEN, CONV_WIDTH, LRU_WIDTH), CONV_WIDTH)
    lru_conv_b = small((N_EVEN, LRU_WIDTH))
    lru_wa = nrm((N_EVEN, LRU_HEADS, LRU_BLOCK, LRU_BLOCK), LRU_BLOCK)
    lru_ba = small((N_EVEN, LRU_WIDTH))
    lru_wx = nrm((N_EVEN, LRU_HEADS, LRU_BLOCK, LRU_BLOCK), LRU_BLOCK)
    lru_bx = small((N_EVEN, LRU_WIDTH))
    a_pow = jax.random.uniform(next(ks), (N_EVEN, LRU_WIDTH), f32, minval=0.9, maxval=0.999)
    s_a = a_pow ** (1.0 / LRU_C)
    lru_lambda = jnp.log(s_a) - jnp.log1p(-s_a)
    ssd_conv_w = nrm((N_EVEN, CONV_WIDTH, SSD_CONV_DIM), CONV_WIDTH)
    ssd_conv_b = small((N_EVEN, SSD_CONV_DIM))
    dt0 = jnp.exp(jax.random.uniform(next(ks), (N_EVEN, SSD_HEADS), f32,
                                     minval=float(np.log(1e-3)), maxval=float(np.log(1e-1))))
    ssd_dt_bias = dt0 + jnp.log(-jnp.expm1(-dt0))
    ssd_a_log = jnp.log(jax.random.uniform(next(ks), (N_EVEN, SSD_HEADS), f32, minval=1.0, maxval=16.0))
    ssd_d = gain((N_EVEN, SSD_HEADS))
    ssd_norm_g = gain((N_EVEN, SSD_WIDTH))
    mlstm_w_up = nrm((N_ODD, D_MODEL, 2 * MLSTM_WIDTH), D_MODEL)
    mlstm_conv_w = nrm((N_ODD, CONV_WIDTH, MLSTM_WIDTH), CONV_WIDTH)
    mlstm_conv_b = small((N_ODD, MLSTM_WIDTH))
    nblk = MLSTM_WIDTH // MLSTM_QKV_BLOCK
    mlstm_wq = nrm((N_ODD, nblk, MLSTM_QKV_BLOCK, MLSTM_QKV_BLOCK), MLSTM_QKV_BLOCK)
    mlstm_wk = nrm((N_ODD, nblk, MLSTM_QKV_BLOCK, MLSTM_QKV_BLOCK), MLSTM_QKV_BLOCK)
    mlstm_wv = nrm((N_ODD, nblk, MLSTM_QKV_BLOCK, MLSTM_QKV_BLOCK), MLSTM_QKV_BLOCK)
    mlstm_w_gates = nrm((N_ODD, 3 * MLSTM_WIDTH, 2 * MLSTM_HEADS), 3 * MLSTM_WIDTH, 0.5)
    f_bias = jnp.broadcast_to(jnp.linspace(3.0, 6.0, MLSTM_HEADS, dtype=f32), (N_ODD, MLSTM_HEADS))
    mlstm_b_gates = jnp.concatenate([small((N_ODD, MLSTM_HEADS), 0.1),
                                     f_bias + small((N_ODD, MLSTM_HEADS), 0.02)], axis=-1)
    mlstm_norm_g = gain((N_ODD, MLSTM_WIDTH))
    mlstm_skip = gain((N_ODD, MLSTM_WIDTH))
    mlstm_w_down = nrm((N_ODD, MLSTM_WIDTH, D_MODEL), MLSTM_WIDTH)
    final_norm_g = gain((D_MODEL,))
    return {'x': x, 'c': c, 'ada_w': ada_w, 'ada_b': ada_b, 'norm_g': norm_g,
            'ffn_w_gate': ffn_w_gate, 'ffn_w_up': ffn_w_up, 'ffn_w_down': ffn_w_down,
            'hyb_w_in': hyb_w_in, 'hyb_w_out': hyb_w_out,
            'lru_conv_w': lru_conv_w, 'lru_conv_b': lru_conv_b, 'lru_wa': lru_wa, 'lru_ba': lru_ba,
            'lru_wx': lru_wx, 'lru_bx': lru_bx, 'lru_lambda': lru_lambda,
            'ssd_conv_w': ssd_conv_w, 'ssd_conv_b': ssd_conv_b, 'ssd_dt_bias': ssd_dt_bias,
            'ssd_a_log': ssd_a_log, 'ssd_d': ssd_d, 'ssd_norm_g': ssd_norm_g,
            'mlstm_w_up': mlstm_w_up, 'mlstm_conv_w': mlstm_conv_w, 'mlstm_conv_b': mlstm_conv_b,
            'mlstm_wq': mlstm_wq, 'mlstm_wk': mlstm_wk, 'mlstm_wv': mlstm_wv,
            'mlstm_w_gates': mlstm_w_gates, 'mlstm_b_gates': mlstm_b_gates,
            'mlstm_norm_g': mlstm_norm_g, 'mlstm_skip': mlstm_skip, 'mlstm_w_down': mlstm_w_down,
            'final_norm_g': final_norm_g}


def reference(x, c, ada_w, ada_b, norm_g, ffn_w_gate, ffn_w_up, ffn_w_down, hyb_w_in, hyb_w_out,
              lru_conv_w, lru_conv_b, lru_wa, lru_ba, lru_wx, lru_bx, lru_lambda,
              ssd_conv_w, ssd_conv_b, ssd_dt_bias, ssd_a_log, ssd_d, ssd_norm_g,
              mlstm_w_up, mlstm_conv_w, mlstm_conv_b, mlstm_wq, mlstm_wk, mlstm_wv,
              mlstm_w_gates, mlstm_b_gates, mlstm_norm_g, mlstm_skip, mlstm_w_down, final_norm_g):
    bsz = x.shape[0]
    c_act = jax.nn.silu(c)
    for layer in range(DEPTH):
        mod = (c_act @ ada_w[layer] + ada_b[layer]).reshape(bsz, N_SUB, 3, D_MODEL)[:, :, :, None, :]

        def modulate(t, sub):
            return rms_norm(t, norm_g[layer, sub]) * (1.0 + mod[:, sub, 1]) + mod[:, sub, 0]

        h = modulate(x, 0)
        x = x + HALF_STEP * (1.0 + mod[:, 0, 2]) * swiglu(h, ffn_w_gate[layer, 0], ffn_w_up[layer, 0], ffn_w_down[layer, 0])
        h = modulate(x, 1)
        if layer % 2 == 0:
            e = layer // 2
            y = hybrid_mixer(h, hyb_w_in[e], hyb_w_out[e], lru_conv_w[e], lru_conv_b[e], lru_wa[e], lru_ba[e],
                             lru_wx[e], lru_bx[e], lru_lambda[e], ssd_conv_w[e], ssd_conv_b[e],
                             ssd_dt_bias[e], ssd_a_log[e], ssd_d[e], ssd_norm_g[e])
        else:
            o = layer // 2
            y = mlstm_block(h, mlstm_w_up[o], mlstm_conv_w[o], mlstm_conv_b[o], mlstm_wq[o], mlstm_wk[o],
                            mlstm_wv[o], mlstm_w_gates[o], mlstm_b_gates[o], mlstm_norm_g[o],
                            mlstm_skip[o], mlstm_w_down[o])
        x = x + (1.0 + mod[:, 1, 2]) * y.astype(x.dtype)
        h = modulate(x, 2)
        x = x + HALF_STEP * (1.0 + mod[:, 2, 2]) * swiglu(h, ffn_w_gate[layer, 1], ffn_w_up[layer, 1], ffn_w_down[layer, 1])
    return rms_norm(x, final_norm_g)
```

```python
import functools
import math

import jax
import jax.numpy as jnp
from jax import lax
from jax.experimental import pallas as pl
from jax.experimental.pallas import tpu as pltpu

F32 = jnp.float32
BF16 = jnp.bfloat16

D_MODEL = 1024
N_SUB = 3
HALF_STEP = 0.5
D_FF = 2816
CONV_WIDTH = 4
EPS = 1e-6
LRU_WIDTH = D_MODEL
LRU_HEADS = 8
LRU_BLOCK = LRU_WIDTH // LRU_HEADS
LRU_C = 8.0
SSD_WIDTH = D_MODEL
SSD_HEADDIM = 64
SSD_HEADS = SSD_WIDTH // SSD_HEADDIM
SSD_GROUPS = 2
SSD_STATE = 128
SSD_CHUNK = 128
SSD_CONV_DIM = SSD_WIDTH + 2 * SSD_GROUPS * SSD_STATE
HYB_SPLITS = (LRU_WIDTH, 2 * LRU_WIDTH, 2 * LRU_WIDTH + SSD_WIDTH, 2 * LRU_WIDTH + SSD_WIDTH + SSD_CONV_DIM)
MLSTM_WIDTH = 2 * D_MODEL
MLSTM_HEADS = 4
MLSTM_HEADDIM = MLSTM_WIDTH // MLSTM_HEADS
MLSTM_QKV_BLOCK = 4

LANE = 128
SUBLANE = 8
MXU_DIM = 256
VMEM_LIMIT_BYTES = 56 * 1024 * 1024

FFN_TILE = 512
HYB_TILE = 512
MLSTM_CHUNK = 256
ADA_TILE_N = 1536
CONV_PAD = SUBLANE


def _silu(x):
    return x * jax.nn.sigmoid(x)


def _softplus(x):
    return jnp.maximum(x, 0.0) + jnp.log1p(jnp.exp(-jnp.abs(x)))


def _gelu_tanh(x):
    c = math.sqrt(2.0 / math.pi)
    return x * (0.5 * (1.0 + jnp.tanh(c * (x + 0.044715 * (x * x * x)))))


def _modulate(x, g, shift, scale):
    y = x * lax.rsqrt(jnp.mean(x * x, axis=-1, keepdims=True) + EPS)
    return (y * g) * (1.0 + scale) + shift


def _split_bf16(x, n):
    parts = []
    r = x
    for _ in range(n):
        p = r.astype(BF16)
        parts.append(p)
        r = r - p.astype(F32)
    return parts


def _dot(a, b):
    return jnp.dot(a, b, preferred_element_type=F32)


def _dot_lhs01(m01, x, n):
    out = None
    for p in _split_bf16(x, n):
        t = _dot(m01, p)
        out = t if out is None else out + t
    return out


def _dot_rhs01(x, m01, n):
    out = None
    for p in _split_bf16(x, n):
        t = _dot(p, m01)
        out = t if out is None else out + t
    return out


def _causal_conv(ext_ref, raw, w_ref, b_ref, t):
    ext_ref[CONV_PAD:CONV_PAD + t, :] = raw
    y = b_ref[...]
    for k in range(CONV_WIDTH):
        off = CONV_PAD - (CONV_WIDTH - 1) + k
        y = y + w_ref[k:k + 1, :] * ext_ref[pl.ds(off, t), :]
    ext_ref[0:CONV_PAD, :] = ext_ref[t:t + CONV_PAD, :]
    return y


def _const_spec(shape):
    nd = len(shape)
    return pl.BlockSpec(shape, lambda *_: (0,) * nd, pipeline_mode=pl.Buffered(1))


def _tile_spec(t, d):
    return pl.BlockSpec((1, t, d), lambda b, i: (b, i, 0))


def _mod_spec(d):
    return pl.BlockSpec((1, 3, d), lambda b, i: (b, 0, 0))


def _params(semantics):
    return pltpu.CompilerParams(dimension_semantics=semantics, vmem_limit_bytes=VMEM_LIMIT_BYTES)


def _ada_kernel(c_ref, w_ref, b_ref, o_ref):
    c = c_ref[...]
    ca = _silu(c)
    w = w_ref[0]
    c1, c2 = _split_bf16(ca, 2)
    w1, w2 = _split_bf16(w, 2)
    o_ref[0] = _dot(c1, w1) + (_dot(c1, w2) + _dot(c2, w1)) + b_ref[0]


def _ada_call(c, ada_w, ada_b):
    depth, d, n = ada_w.shape
    bsz = c.shape[0]
    rows = -(-bsz // SUBLANE) * SUBLANE
    c_pad = jnp.zeros((rows, d), F32).at[:bsz].set(c)
    out = pl.pallas_call(
        _ada_kernel,
        grid=(depth, n // ADA_TILE_N),
        in_specs=[pl.BlockSpec((rows, d), lambda l, j: (0, 0)),
                  pl.BlockSpec((1, d, ADA_TILE_N), lambda l, j: (l, 0, j)),
                  pl.BlockSpec((1, 1, ADA_TILE_N), lambda l, j: (l, 0, j))],
        out_specs=pl.BlockSpec((1, rows, ADA_TILE_N), lambda l, j: (l, 0, j)),
        out_shape=jax.ShapeDtypeStruct((depth, rows, n), F32),
        compiler_params=_params(("arbitrary", "arbitrary")),
        name="ada_mod",
    )(c_pad, ada_w, ada_b.reshape(depth, 1, n))
    return out[:, :bsz].reshape(depth, bsz, N_SUB, 3, d)


def _ffn_kernel(x_ref, mod_ref, g_ref, wg_ref, wu_ref, wd_ref, fg_ref, o_ref, *, final_norm):
    x = x_ref[0]
    m = mod_ref[0]
    h = _modulate(x, g_ref[...], m[0:1], m[1:2]).astype(BF16)
    gate = _dot(h, wg_ref[...])
    up = _dot(h, wu_ref[...])
    act = (_silu(gate) * up).astype(BF16)
    y = _dot(act, wd_ref[...])
    o = x + (HALF_STEP * (1.0 + m[2:3])) * y
    if final_norm:
        o = (o * lax.rsqrt(jnp.mean(o * o, axis=-1, keepdims=True) + EPS)) * fg_ref[...]
    o_ref[0] = o


def _ffn_call(x, mod3, g, wg, wu, wd, final_g, *, final_norm):
    bsz, seq, d = x.shape
    f = wg.shape[1]
    t = FFN_TILE
    return pl.pallas_call(
        functools.partial(_ffn_kernel, final_norm=final_norm),
        grid=(bsz, seq // t),
        in_specs=[_tile_spec(t, d), _mod_spec(d), _const_spec((1, d)),
                  _const_spec((d, f)), _const_spec((d, f)), _const_spec((f, d)), _const_spec((1, d))],
        out_specs=_tile_spec(t, d),
        out_shape=jax.ShapeDtypeStruct((bsz, seq, d), F32),
        compiler_params=_params(("arbitrary", "arbitrary")),
        name="ffn",
    )(x, mod3, g.reshape(1, d), wg.astype(BF16), wu.astype(BF16), wd.astype(BF16), final_g.reshape(1, d))


def _lru_group_scan(a, u):
    row = lax.broadcasted_iota(jnp.int32, a.shape, 0)
    for d in (1, 2, 4):
        a_prev = pltpu.roll(a, d, 0)
        u_prev = pltpu.roll(u, d, 0)
        keep = row >= d
        u = jnp.where(keep, a * u_prev + u, u)
        a = jnp.where(keep, a * a_prev, a)
    return a, u


def _hyb_kernel(x_ref, mod_ref, g_ref, win_ref, lcw_ref, lcb_ref, wai_ref, ba_ref, bx_ref, lam_ref,
                scw_ref, scb_ref, dtb_ref, alog_ref, dexp_ref, sng_ref, e_ref, wout_ref,
                o_ref,
                ext_l, ext_s, a_s, u_s, hcar, xbc_s, dt_s, z_s, sstate, ycat):
    t = x_ref.shape[1]
    ln = SSD_CHUNK

    @pl.when(pl.program_id(1) == 0)
    def _():
        ext_l[0:CONV_PAD, :] = jnp.zeros((CONV_PAD, LRU_WIDTH), F32)
        ext_s[0:CONV_PAD, :] = jnp.zeros((CONV_PAD, SSD_CONV_DIM), F32)
        hcar[...] = jnp.zeros_like(hcar)
        sstate[...] = jnp.zeros_like(sstate)

    x = x_ref[0]
    m = mod_ref[0]
    h = _modulate(x, g_ref[...], m[0:1], m[1:2]).astype(BF16)
    s0, s1, s2, s3 = HYB_SPLITS
    gate_l = _dot(h, win_ref[:, 0:s0])
    xl_raw = _dot(h, win_ref[:, s0:s1])
    z_s[...] = _dot(h, win_ref[:, s1:s2])
    xbc_raw = _dot(h, win_ref[:, s2:s3])
    dt_raw = _dot(h, win_ref[:, s3:s3 + LANE])

    xl = _causal_conv(ext_l, xl_raw, lcw_ref, lcb_ref, t)
    xl_b = xl.astype(BF16)
    for hh in range(LRU_HEADS):
        sl = slice(LRU_BLOCK * hh, LRU_BLOCK * (hh + 1))
        ri = _dot(xl_b[:, sl], wai_ref[hh])
        r = jax.nn.sigmoid(ri[:, :LRU_BLOCK] + ba_ref[:, sl])
        ig = jax.nn.sigmoid(ri[:, LRU_BLOCK:] + bx_ref[:, sl])
        log_a = (-LRU_C * r) * _softplus(-lam_ref[:, sl])
        a = jnp.exp(log_a)
        a_s[:, sl] = a
        u_s[:, sl] = jnp.sqrt(-jnp.tanh(log_a) * (a * a + 1.0)) * (ig * xl[:, sl])

    def scan_body(gi, carry):
        rows = pl.ds(pl.multiple_of(gi * SUBLANE, SUBLANE), SUBLANE)
        a8, u8 = _lru_group_scan(a_s[rows, :], u_s[rows, :])
        h8 = a8 * carry + u8
        u_s[rows, :] = h8
        return h8[SUBLANE - 1:SUBLANE, :]

    hcar[...] = lax.fori_loop(0, t // SUBLANE, scan_body, hcar[...])
    ycat[:, 0:LRU_WIDTH] = (u_s[...] * _gelu_tanh(gate_l)).astype(BF16)

    xbc = _causal_conv(ext_s, xbc_raw, scw_ref, scb_ref, t)
    xbc_s[...] = _silu(xbc)
    dt_s[...] = _softplus(dt_raw + dtb_ref[...])

    gn = SSD_GROUPS * SSD_STATE
    gw = SSD_WIDTH // SSD_GROUPS
    hpg = SSD_HEADS // SSD_GROUPS
    ri_ = lax.broadcasted_iota(jnp.int32, (ln, ln), 0)
    ci_ = lax.broadcasted_iota(jnp.int32, (ln, ln), 1)
    causal = ci_ <= ri_
    tril = jnp.where(causal, 1.0, 0.0).astype(BF16)
    lane = lax.broadcasted_iota(jnp.int32, (ln, LANE), 1)
    low_half = lane < SSD_HEADDIM

    def chunk_body(c, _):
        rows = pl.ds(pl.multiple_of(c * ln, ln), ln)
        xs = xbc_s[rows, 0:SSD_WIDTH]
        bm = xbc_s[rows, SSD_WIDTH:SSD_WIDTH + gn]
        cm = xbc_s[rows, SSD_WIDTH + gn:SSD_WIDTH + 2 * gn]
        dt = dt_s[rows, :]
        ac = dt * (-jnp.exp(alog_ref[...]))
        acs = _dot_lhs01(tril, ac, 3)
        acs_last = acs[ln - 1:ln, :]
        acs_t = acs.T
        dt_t = dt.T
        e = e_ref[...]
        dec_out = _dot_rhs01(jnp.exp(acs), e, 2)
        w_st = _dot_rhs01(dt * jnp.exp(acs_last - acs), e, 2)
        d_a = _dot_rhs01(jnp.broadcast_to(jnp.exp(acs_last), (SUBLANE, LANE)), e, 2)[0:1]
        xw_b = (xs * w_st).astype(BF16)
        s_prev = sstate[...]
        s_prev_b = s_prev.astype(BF16)
        ys = []
        sts = []
        for g in range(SSD_GROUPS):
            bg = bm[:, SSD_STATE * g:SSD_STATE * (g + 1)]
            cg_b = cm[:, SSD_STATE * g:SSD_STATE * (g + 1)].astype(BF16)
            cb = lax.dot_general(cg_b, bg.astype(BF16), (((1,), (1,)), ((), ())),
                                 preferred_element_type=F32)
            y_off = _dot(cg_b, s_prev_b[:, gw * g:gw * (g + 1)])
            sts.append(_dot(bg.T.astype(BF16), xw_b[:, gw * g:gw * (g + 1)]))
            for j in range(hpg // 2):
                pair = (hpg // 2) * g + j
                mats = []
                for hd in (2 * pair, 2 * pair + 1):
                    diff = jnp.where(causal, acs[:, hd:hd + 1] - acs_t[hd:hd + 1, :], -jnp.inf)
                    mats.append(cb * jnp.exp(diff) * dt_t[hd:hd + 1, :])
                lhs = jnp.concatenate(mats, axis=1).astype(BF16)
                xp = xs[:, LANE * pair:LANE * (pair + 1)]
                rhs = jnp.concatenate([jnp.where(low_half, xp, 0.0), jnp.where(low_half, 0.0, xp)],
                                      axis=0).astype(BF16)
                ys.append(_dot(lhs, rhs)
                          + y_off[:, LANE * j:LANE * (j + 1)] * dec_out[:, LANE * pair:LANE * (pair + 1)])
        sstate[...] = s_prev * d_a + jnp.concatenate(sts, axis=1)
        y = jnp.concatenate(ys, axis=1) + dexp_ref[...] * xs
        y = y * _silu(z_s[rows, :])
        for g in range(SSD_GROUPS):
            yg = y[:, gw * g:gw * (g + 1)]
            yn = (yg * lax.rsqrt(jnp.mean(yg * yg, axis=-1, keepdims=True) + EPS)) * sng_ref[:, gw * g:gw * (g + 1)]
            ycat[rows, LRU_WIDTH + gw * g:LRU_WIDTH + gw * (g + 1)] = yn.astype(BF16)
        return 0

    lax.fori_loop(0, t // ln, chunk_body, 0)

    y_mix = _dot(ycat[...], wout_ref[...])
    o_ref[0] = x + (1.0 + m[2:3]) * y_mix


def _hyb_call(x, mod3, g, w_in, w_out, lru_conv_w, lru_conv_b, lru_wa, lru_ba, lru_wx, lru_bx, lru_lambda,
              ssd_conv_w, ssd_conv_b, ssd_dt_bias, ssd_a_log, ssd_d, ssd_norm_g):
    bsz, seq, d = x.shape
    t = HYB_TILE
    n_in = HYB_SPLITS[-1] + LANE
    w_in_p = jnp.zeros((d, n_in), BF16).at[:, :w_in.shape[1]].set(w_in.astype(BF16))
    wai = jnp.concatenate([lru_wa, lru_wx], axis=-1).astype(BF16)
    pad_h = lambda v: jnp.zeros((1, LANE), F32).at[0, :SSD_HEADS].set(v)
    expand = jnp.zeros((LANE, SSD_WIDTH), F32).at[:SSD_HEADS].set(
        jnp.repeat(jnp.eye(SSD_HEADS, dtype=F32), SSD_HEADDIM, axis=1)).astype(BF16)
    row = lambda v: v.reshape(1, -1)
    args = (x, mod3, row(g), w_in_p, lru_conv_w, row(lru_conv_b), wai, row(lru_ba), row(lru_bx), row(lru_lambda),
            ssd_conv_w, row(ssd_conv_b), pad_h(ssd_dt_bias), pad_h(ssd_a_log),
            row(jnp.repeat(ssd_d, SSD_HEADDIM)), row(ssd_norm_g), expand, w_out.astype(BF16))
    in_specs = [_tile_spec(t, d), _mod_spec(d)] + [_const_spec(a.shape) for a in args[2:]]
    scratch = [
        pltpu.VMEM((t + CONV_PAD, LRU_WIDTH), F32),
        pltpu.VMEM((t + CONV_PAD, SSD_CONV_DIM), F32),
        pltpu.VMEM((t, LRU_WIDTH), F32),
        pltpu.VMEM((t, LRU_WIDTH), F32),
        pltpu.VMEM((1, LRU_WIDTH), F32),
        pltpu.VMEM((t, SSD_CONV_DIM), F32),
        pltpu.VMEM((t, LANE), F32),
        pltpu.VMEM((t, SSD_WIDTH), F32),
        pltpu.VMEM((SSD_STATE, SSD_WIDTH), F32),
        pltpu.VMEM((t, LRU_WIDTH + SSD_WIDTH), BF16),
    ]
    return pl.pallas_call(
        _hyb_kernel,
        grid=(bsz, seq // t),
        in_specs=in_specs,
        out_specs=_tile_spec(t, d),
        out_shape=jax.ShapeDtypeStruct((bsz, seq, d), F32),
        scratch_shapes=scratch,
        compiler_params=_params(("arbitrary", "arbitrary")),
        name="hybrid_mixer",
    )(*args)


def _mlstm_kernel(x_ref, mod_ref, g_ref, wup_ref, cw_ref, cb_ref, wqk_ref, wv_ref, wg_ref, bg_ref,
                  ng_ref, skip_ref, wdown_ref,
                  o_ref,
                  ext, q_s, k_s, v_s, cst, nst, mst, ycat):
    ln = x_ref.shape[1]
    w = MLSTM_WIDTH
    dh = MLSTM_HEADDIM
    nblk = w // MXU_DIM

    @pl.when(pl.program_id(1) == 0)
    def _():
        ext[0:CONV_PAD, :] = jnp.zeros((CONV_PAD, w), F32)
        cst[...] = jnp.zeros_like(cst)
        nst[...] = jnp.zeros_like(nst)
        mst[...] = jnp.zeros_like(mst)

    x = x_ref[0]
    m = mod_ref[0]
    h = _modulate(x, g_ref[...], m[0:1], m[1:2]).astype(BF16)
    xm = _dot(h, wup_ref[:, 0:w])
    z = _dot(h, wup_ref[:, w:2 * w])
    xc = _silu(_causal_conv(ext, xm, cw_ref, cb_ref, ln))
    xc_b = xc.astype(BF16)
    xm_b = xm.astype(BF16)
    for b in range(nblk):
        sl = slice(MXU_DIM * b, MXU_DIM * (b + 1))
        qk = _dot(xc_b[:, sl], wqk_ref[b])
        q_s[:, sl] = qk[:, :MXU_DIM]
        k_s[:, sl] = qk[:, MXU_DIM:]
        v_s[:, sl] = _dot(xm_b[:, sl], wv_ref[b])
    gates = (_dot(q_s[...].astype(BF16), wg_ref[0:w, :]) + _dot(k_s[...].astype(BF16), wg_ref[w:2 * w, :])
             + _dot(v_s[...].astype(BF16), wg_ref[2 * w:3 * w, :]) + bg_ref[...])
    log_f = -_softplus(-gates)
    ri_ = lax.broadcasted_iota(jnp.int32, (ln, ln), 0)
    ci_ = lax.broadcasted_iota(jnp.int32, (ln, ln), 1)
    causal = ci_ <= ri_
    tril = jnp.where(causal, 1.0, 0.0).astype(BF16)
    bcum = _dot_lhs01(tril, log_f, 3)
    gates_t = gates.T
    bcum_t = bcum.T
    k_scale = dh ** -0.5
    for hd in range(MLSTM_HEADS):
        sl = slice(dh * hd, dh * (hd + 1))
        fi = MLSTM_HEADS + hd
        bc_col = bcum[:, fi:fi + 1]
        bc_row = bcum_t[fi:fi + 1, :]
        i_col = gates[:, hd:hd + 1]
        i_row = gates_t[hd:hd + 1, :]
        m_prev = mst[hd][:, 0:1]
        log_d = jnp.where(causal, bc_col - bc_row + i_row, -jnp.inf)
        g_col = bc_col + m_prev
        m_new_rows = jnp.maximum(g_col, jnp.max(log_d, axis=-1, keepdims=True))
        w_inter = jnp.exp(g_col - m_new_rows)
        q_h = q_s[:, sl]
        k_h = k_s[:, sl] * k_scale
        q_b = q_h.astype(BF16)
        v_b = v_s[:, sl].astype(BF16)
        s_mat = lax.dot_general(q_b, k_h.astype(BF16), (((1,), (1,)), ((), ())),
                                preferred_element_type=F32) * jnp.exp(log_d - m_new_rows)
        c_prev = cst[hd]
        n_prev = nst[hd]
        num = w_inter * _dot(q_b, c_prev.astype(BF16)) + _dot(s_mat.astype(BF16), v_b)
        den = (w_inter * jnp.sum(q_h * n_prev, axis=-1, keepdims=True)
               + jnp.sum(s_mat, axis=-1, keepdims=True))
        h_out = num / jnp.maximum(jnp.abs(den), jnp.exp(-m_new_rows))
        b_last = bc_col[ln - 1:ln, :]
        log_w = b_last - bc_col + i_col
        m_next = jnp.maximum(b_last + m_prev, jnp.max(log_w, axis=0, keepdims=True))
        w_s = jnp.exp(log_w - m_next)
        decay = jnp.exp(b_last + m_prev - m_next)
        kw = k_h * w_s
        cst[hd] = decay * c_prev + lax.dot_general(kw.astype(BF16), v_b, (((0,), (0,)), ((), ())),
                                                   preferred_element_type=F32)
        nst[hd] = decay * n_prev + jnp.sum(kw, axis=0, keepdims=True)
        mst[hd] = jnp.broadcast_to(m_next, (1, LANE))
        mu = jnp.mean(h_out, axis=-1, keepdims=True)
        hc = h_out - mu
        var = jnp.mean(hc * hc, axis=-1, keepdims=True)
        hn = (hc * lax.rsqrt(var + EPS)) * ng_ref[:, sl]
        out = (hn + skip_ref[:, sl] * xc[:, sl]) * _silu(z[:, sl])
        ycat[:, sl] = out.astype(BF16)

    y_mix = _dot(ycat[...], wdown_ref[...])
    o_ref[0] = x + (1.0 + m[2:3]) * y_mix


def _expand_block_diag(w, group):
    nb, bi, bo = w.shape
    wg = w.reshape(nb // group, group, bi, bo)
    eye = jnp.eye(group, dtype=w.dtype)
    return (wg[:, :, :, None, :] * eye[None, :, None, :, None]).reshape(nb // group, group * bi, group * bo)


def _mlstm_call(x, mod3, g, w_up, conv_w, conv_b, wq, wk, wv, w_gates, b_gates, norm_g, skip, w_down):
    bsz, seq, d = x.shape
    ln = MLSTM_CHUNK
    w = MLSTM_WIDTH
    grp = MXU_DIM // MLSTM_QKV_BLOCK
    wqk = jnp.concatenate([_expand_block_diag(wq, grp), _expand_block_diag(wk, grp)], axis=-1).astype(BF16)
    wvd = _expand_block_diag(wv, grp).astype(BF16)
    wg_p = jnp.zeros((3 * w, LANE), BF16).at[:, :2 * MLSTM_HEADS].set(w_gates.astype(BF16))
    bg_p = jnp.zeros((1, LANE), F32).at[0, :2 * MLSTM_HEADS].set(b_gates)
    row = lambda v: v.reshape(1, -1)
    args = (x, mod3, row(g), w_up.astype(BF16), conv_w, row(conv_b), wqk, wvd, wg_p, bg_p,
            row(norm_g), row(skip), w_down.astype(BF16))
    in_specs = [_tile_spec(ln, d), _mod_spec(d)] + [_const_spec(a.shape) for a in args[2:]]
    scratch = [
        pltpu.VMEM((ln + CONV_PAD, w), F32),
        pltpu.VMEM((ln, w), F32),
        pltpu.VMEM((ln, w), F32),
        pltpu.VMEM((ln, w), F32),
        pltpu.VMEM((MLSTM_HEADS, MLSTM_HEADDIM, MLSTM_HEADDIM), F32),
        pltpu.VMEM((MLSTM_HEADS, 1, MLSTM_HEADDIM), F32),
        pltpu.VMEM((MLSTM_HEADS, 1, LANE), F32),
        pltpu.VMEM((ln, w), BF16),
    ]
    return pl.pallas_call(
        _mlstm_kernel,
        grid=(bsz, seq // ln),
        in_specs=in_specs,
        out_specs=_tile_spec(ln, d),
        out_shape=jax.ShapeDtypeStruct((bsz, seq, d), F32),
        scratch_shapes=scratch,
        compiler_params=_params(("arbitrary", "arbitrary")),
        name="mlstm_mixer",
    )(*args)


def kernel(x, c, ada_w, ada_b, norm_g, ffn_w_gate, ffn_w_up, ffn_w_down, hyb_w_in, hyb_w_out, lru_conv_w, lru_conv_b, lru_wa, lru_ba, lru_wx, lru_bx, lru_lambda, ssd_conv_w, ssd_conv_b, ssd_dt_bias, ssd_a_log, ssd_d, ssd_norm_g, mlstm_w_up, mlstm_conv_w, mlstm_conv_b, mlstm_wq, mlstm_wk, mlstm_wv, mlstm_w_gates, mlstm_b_gates, mlstm_norm_g, mlstm_skip, mlstm_w_down, final_norm_g):
    depth = ada_w.shape[0]
    mod = _ada_call(c, ada_w, ada_b)
    for layer in range(depth):
        x = _ffn_call(x, mod[layer, :, 0], norm_g[layer, 0], ffn_w_gate[layer, 0], ffn_w_up[layer, 0],
                      ffn_w_down[layer, 0], final_norm_g, final_norm=False)
        if layer % 2 == 0:
            e = layer // 2
            x = _hyb_call(x, mod[layer, :, 1], norm_g[layer, 1], hyb_w_in[e], hyb_w_out[e], lru_conv_w[e],
                          lru_conv_b[e], lru_wa[e], lru_ba[e], lru_wx[e], lru_bx[e], lru_lambda[e],
                          ssd_conv_w[e], ssd_conv_b[e], ssd_dt_bias[e], ssd_a_log[e], ssd_d[e], ssd_norm_g[e])
        else:
            o = layer // 2
            x = _mlstm_call(x, mod[layer, :, 1], norm_g[layer, 1], mlstm_w_up[o], mlstm_conv_w[o], mlstm_conv_b[o],
                            mlstm_wq[o], mlstm_wk[o], mlstm_wv[o], mlstm_w_gates[o], mlstm_b_gates[o],
                            mlstm_norm_g[o], mlstm_skip[o], mlstm_w_down[o])
        x = _ffn_call(x, mod[layer, :, 2], norm_g[layer, 2], ffn_w_gate[layer, 1], ffn_w_up[layer, 1],
                      ffn_w_down[layer, 1], final_norm_g, final_norm=(layer == depth - 1))
    return x
```

```python
import functools
import math

import jax
import jax.numpy as jnp
from jax import lax
from jax.experimental import pallas as pl
from jax.experimental.pallas import tpu as pltpu

F32 = jnp.float32
BF16 = jnp.bfloat16

D_MODEL = 1024
N_SUB = 3
HALF_STEP = 0.5
D_FF = 2816
CONV_WIDTH = 4
EPS = 1e-6
LRU_WIDTH = D_MODEL
LRU_HEADS = 8
LRU_BLOCK = LRU_WIDTH // LRU_HEADS
LRU_C = 8.0
SSD_WIDTH = D_MODEL
SSD_HEADDIM = 64
SSD_HEADS = SSD_WIDTH // SSD_HEADDIM
SSD_GROUPS = 2
SSD_STATE = 128
SSD_CHUNK = 128
SSD_CONV_DIM = SSD_WIDTH + 2 * SSD_GROUPS * SSD_STATE
HYB_SPLITS = (LRU_WIDTH, 2 * LRU_WIDTH, 2 * LRU_WIDTH + SSD_WIDTH, 2 * LRU_WIDTH + SSD_WIDTH + SSD_CONV_DIM)
MLSTM_WIDTH = 2 * D_MODEL
MLSTM_HEADS = 4
MLSTM_HEADDIM = MLSTM_WIDTH // MLSTM_HEADS
MLSTM_QKV_BLOCK = 4

LANE = 128
SUBLANE = 8
PACKED_ROWS = 16
MXU_DIM = 256
VMEM_LIMIT_BYTES = 56 * 1024 * 1024

FFN_TILE = 512
HYB_TILE = 512
MLSTM_CHUNK = 256
ADA_TILE_N = 1536
CAST_ROWS = 512
CONV_PAD = SUBLANE
ROW_STRIP = PACKED_ROWS
CONV_STRIP = 64
LRU_COLS = 2 * LRU_BLOCK
SSD_COLS = 512


def _silu(x):
    return x * jax.nn.sigmoid(x)


def _softplus(x):
    return jnp.maximum(x, 0.0) + jnp.log1p(jnp.exp(-jnp.abs(x)))


def _gelu_tanh(x):
    c = math.sqrt(2.0 / math.pi)
    return x * (0.5 * (1.0 + jnp.tanh(c * (x + 0.044715 * (x * x * x)))))


def _modulate(x, g, shift, scale):
    y = x * lax.rsqrt(jnp.mean(x * x, axis=-1, keepdims=True) + EPS)
    return (y * g) * (1.0 + scale) + shift


def _modulate_rows(x_ref, h_ref, g, shift, scale, t):
    for r in range(t // ROW_STRIP):
        rows = slice(ROW_STRIP * r, ROW_STRIP * (r + 1))
        h_ref[rows, :] = _modulate(x_ref[0, rows, :], g, shift, scale).astype(BF16)


def _split_bf16(x, n):
    parts = []
    r = x
    for _ in range(n):
        p = r.astype(BF16)
        parts.append(p)
        r = r - p.astype(F32)
    return parts


def _dot(a, b):
    return jnp.dot(a, b, preferred_element_type=F32)


def _dot_lhs01(m01, x, n):
    out = None
    for p in _split_bf16(x, n):
        t = _dot(m01, p)
        out = t if out is None else out + t
    return out


def _dot_rhs01(x, m01, n):
    out = None
    for p in _split_bf16(x, n):
        t = _dot(p, m01)
        out = t if out is None else out + t
    return out


def _conv_rows(ext_ref, w_ref, b_ref, r0, nrows, cols):
    ntile = nrows // SUBLANE
    tiles = [ext_ref[r0 + SUBLANE * i:r0 + SUBLANE * (i + 1), cols] for i in range(ntile + 1)]
    row = lax.broadcasted_iota(jnp.int32, tiles[0].shape, 0)
    bias = b_ref[:, cols]
    w_now = w_ref[CONV_WIDTH - 1:CONV_WIDTH, cols]
    acc = [bias + w_now * tiles[i + 1] for i in range(ntile)]
    for s in range(1, CONV_WIDTH):
        w_s = w_ref[CONV_WIDTH - 1 - s:CONV_WIDTH - s, cols]
        rot = [pltpu.roll(tl, s, 0) for tl in tiles]
        head = row < s
        acc = [acc[i] + w_s * jnp.where(head, rot[i], rot[i + 1]) for i in range(ntile)]
    return jnp.concatenate(acc, axis=0)


def _const_spec(shape):
    nd = len(shape)
    return pl.BlockSpec(shape, lambda *_: (0,) * nd, pipeline_mode=pl.Buffered(1))


def _select_spec(block, lead):
    nl = len(lead)
    return pl.BlockSpec((None,) * nl + tuple(block), lambda *_: tuple(lead) + (0,) * len(block),
                        pipeline_mode=pl.Buffered(1))


def _tile_spec(t, d):
    return pl.BlockSpec((1, t, d), lambda b, i: (b, i, 0))


def _mod_spec(d):
    return pl.BlockSpec((1, 3, d), lambda b, i: (b, 0, 0))


def _params(semantics):
    return pltpu.CompilerParams(dimension_semantics=semantics, vmem_limit_bytes=VMEM_LIMIT_BYTES)


def _cast_kernel(w_ref, o_ref):
    o_ref[...] = w_ref[...].astype(BF16)


def _cast_bf16(w):
    shape = w.shape
    cols = shape[-1]
    rows = math.prod(shape[:-1])
    out = pl.pallas_call(
        _cast_kernel,
        grid=(rows // CAST_ROWS,),
        in_specs=[pl.BlockSpec((CAST_ROWS, cols), lambda i: (i, 0))],
        out_specs=pl.BlockSpec((CAST_ROWS, cols), lambda i: (i, 0)),
        out_shape=jax.ShapeDtypeStruct((rows, cols), BF16),
        compiler_params=_params(("arbitrary",)),
        name="cast_bf16",
    )(w.reshape(rows, cols))
    return out.reshape(shape)


def _ada_kernel(c_ref, w_ref, b_ref, o_ref):
    c = c_ref[...]
    ca = _silu(c)
    w = w_ref[0]
    c1, c2 = _split_bf16(ca, 2)
    w1, w2 = _split_bf16(w, 2)
    o_ref[0] = _dot(c1, w1) + (_dot(c1, w2) + _dot(c2, w1)) + b_ref[0]


def _ada_call(c, ada_w, ada_b):
    depth, d, n = ada_w.shape
    bsz = c.shape[0]
    rows = -(-bsz // SUBLANE) * SUBLANE
    c_pad = jnp.zeros((rows, d), F32).at[:bsz].set(c)
    out = pl.pallas_call(
        _ada_kernel,
        grid=(depth, n // ADA_TILE_N),
        in_specs=[pl.BlockSpec((rows, d), lambda l, j: (0, 0)),
                  pl.BlockSpec((1, d, ADA_TILE_N), lambda l, j: (l, 0, j)),
                  pl.BlockSpec((1, 1, ADA_TILE_N), lambda l, j: (l, 0, j))],
        out_specs=pl.BlockSpec((1, rows, ADA_TILE_N), lambda l, j: (l, 0, j)),
        out_shape=jax.ShapeDtypeStruct((depth, rows, n), F32),
        compiler_params=_params(("arbitrary", "arbitrary")),
        name="ada_mod",
    )(c_pad, ada_w, ada_b.reshape(depth, 1, n))
    return out[:, :bsz].reshape(depth, bsz, N_SUB, 3, d)


def _ffn_kernel(x_ref, mod_ref, g_ref, wg_ref, wu_ref, wd_ref, fg_ref, o_ref, *, final_norm):
    x = x_ref[0]
    m = mod_ref[0]
    h = _modulate(x, g_ref[...], m[0:1], m[1:2]).astype(BF16)
    gate = _dot(h, wg_ref[...])
    up = _dot(h, wu_ref[...])
    act = (_silu(gate) * up).astype(BF16)
    y = _dot(act, wd_ref[...])
    o = x + (HALF_STEP * (1.0 + m[2:3])) * y
    if final_norm:
        o = (o * lax.rsqrt(jnp.mean(o * o, axis=-1, keepdims=True) + EPS)) * fg_ref[...]
    o_ref[0] = o


def _ffn_call(x, mod3, g, wg, wu, wd, final_g, *, lead, final_norm):
    bsz, seq, d = x.shape
    f = wg.shape[-1]
    t = FFN_TILE
    return pl.pallas_call(
        functools.partial(_ffn_kernel, final_norm=final_norm),
        grid=(bsz, seq // t),
        in_specs=[_tile_spec(t, d), _mod_spec(d), _const_spec((1, d)),
                  _select_spec((d, f), lead), _select_spec((d, f), lead), _select_spec((f, d), lead),
                  _const_spec((1, d))],
        out_specs=_tile_spec(t, d),
        out_shape=jax.ShapeDtypeStruct((bsz, seq, d), F32),
        compiler_params=_params(("arbitrary", "arbitrary")),
        name="ffn",
    )(x, mod3, g.reshape(1, d), wg, wu, wd, final_g.reshape(1, d))


def _lru_group_scan(a, u):
    row = lax.broadcasted_iota(jnp.int32, a.shape, 0)
    for d in (1, 2, 4):
        a_prev = pltpu.roll(a, d, 0)
        u_prev = pltpu.roll(u, d, 0)
        keep = row >= d
        u = jnp.where(keep, a * u_prev + u, u)
        a = jnp.where(keep, a * a_prev, a)
    return a, u


def _ssd_chunk(c, causal, tril, low_half, xbc_s, dt_s, z_s, sstate, ycat, alog_ref, dexp_ref, sng_ref, e_ref):
    ln = SSD_CHUNK
    gn = SSD_GROUPS * SSD_STATE
    gw = SSD_WIDTH // SSD_GROUPS
    hpg = SSD_HEADS // SSD_GROUPS
    rows = slice(c * ln, (c + 1) * ln)
    xs = xbc_s[rows, 0:SSD_WIDTH]
    bm = xbc_s[rows, SSD_WIDTH:SSD_WIDTH + gn]
    cm = xbc_s[rows, SSD_WIDTH + gn:SSD_WIDTH + 2 * gn]
    dt = dt_s[rows, :]
    ac = dt * (-jnp.exp(alog_ref[...]))
    acs = _dot_lhs01(tril, ac, 3)
    acs_last = acs[ln - 1:ln, :]
    acs_t = acs.T
    dt_t = dt.T
    e = e_ref[...]
    dec_out = _dot_rhs01(jnp.exp(acs), e, 1)
    w_st = _dot_rhs01(dt * jnp.exp(acs_last - acs), e, 1)
    d_a = _dot_rhs01(jnp.broadcast_to(jnp.exp(acs_last), (SUBLANE, LANE)), e, 2)[0:1]
    xw_b = (xs * w_st).astype(BF16)
    s_prev = sstate[...]
    s_prev_b = s_prev.astype(BF16)
    ys = []
    sts = []
    for g in range(SSD_GROUPS):
        bg = bm[:, SSD_STATE * g:SSD_STATE * (g + 1)]
        cg_b = cm[:, SSD_STATE * g:SSD_STATE * (g + 1)].astype(BF16)
        cb = lax.dot_general(cg_b, bg.astype(BF16), (((1,), (1,)), ((), ())),
                             preferred_element_type=F32)
        y_off = _dot(cg_b, s_prev_b[:, gw * g:gw * (g + 1)])
        sts.append(_dot(bg.T.astype(BF16), xw_b[:, gw * g:gw * (g + 1)]))
        for j in range(hpg // 2):
            pair = (hpg // 2) * g + j
            mats = []
            for hd in (2 * pair, 2 * pair + 1):
                diff = jnp.where(causal, acs[:, hd:hd + 1] - acs_t[hd:hd + 1, :], -jnp.inf)
                mats.append(cb * jnp.exp(diff) * dt_t[hd:hd + 1, :])
            lhs = jnp.concatenate(mats, axis=1).astype(BF16)
            xp = xs[:, LANE * pair:LANE * (pair + 1)]
            rhs = jnp.concatenate([jnp.where(low_half, xp, 0.0), jnp.where(low_half, 0.0, xp)],
                                  axis=0).astype(BF16)
            ys.append(_dot(lhs, rhs)
                      + y_off[:, LANE * j:LANE * (j + 1)] * dec_out[:, LANE * pair:LANE * (pair + 1)])
    sstate[...] = s_prev * d_a + jnp.concatenate(sts, axis=1)
    y = jnp.concatenate(ys, axis=1) + dexp_ref[...] * xs
    y = y * _silu(z_s[rows, :])
    for g in range(SSD_GROUPS):
        yg = y[:, gw * g:gw * (g + 1)]
        yn = (yg * lax.rsqrt(jnp.mean(yg * yg, axis=-1, keepdims=True) + EPS)) * sng_ref[:, gw * g:gw * (g + 1)]
        ycat[rows, LRU_WIDTH + gw * g:LRU_WIDTH + gw * (g + 1)] = yn.astype(BF16)


def _hyb_kernel(x_ref, mod_ref, g_ref, win_ref, wdt_ref, lcw_ref, lcb_ref, wai_ref, ba_ref, bx_ref, lam_ref,
                scw_ref, scb_ref, dtb_ref, alog_ref, dexp_ref, sng_ref, e_ref, wout_ref,
                o_ref,
                h_s, ext_l, xl_s, ri_s, gl_s, hcar, ext_s, xbc_s, dt_s, z_s, sstate, ycat):
    t = x_ref.shape[1]
    ln = SSD_CHUNK
    s0, s1, s2, s3 = HYB_SPLITS

    @pl.when(pl.program_id(1) == 0)
    def _():
        ext_l[0:CONV_PAD, :] = jnp.zeros((CONV_PAD, LRU_WIDTH), F32)
        ext_s[0:CONV_PAD, :] = jnp.zeros((CONV_PAD, SSD_CONV_DIM), F32)
        hcar[...] = jnp.zeros_like(hcar)
        sstate[...] = jnp.zeros_like(sstate)

    m = mod_ref[0]
    _modulate_rows(x_ref, h_s, g_ref[...], m[0:1], m[1:2], t)

    for p in range(LRU_WIDTH // LRU_COLS):
        cols = slice(LRU_COLS * p, LRU_COLS * (p + 1))
        rcols = slice(2 * LRU_COLS * p, 2 * LRU_COLS * p + LRU_COLS)
        icols = slice(2 * LRU_COLS * p + LRU_COLS, 2 * LRU_COLS * (p + 1))
        ext_l[CONV_PAD:CONV_PAD + t, cols] = _dot(h_s[...], win_ref[:, s0 + LRU_COLS * p:s0 + LRU_COLS * (p + 1)])
        gl_s[:, cols] = _dot(h_s[...], win_ref[:, LRU_COLS * p:LRU_COLS * (p + 1)])
        for r in range(t // CONV_STRIP):
            xl_s[CONV_STRIP * r:CONV_STRIP * (r + 1), cols] = _conv_rows(ext_l, lcw_ref, lcb_ref, CONV_STRIP * r,
                                                                          CONV_STRIP, cols)
        ext_l[0:CONV_PAD, cols] = ext_l[t:t + CONV_PAD, cols]
        ri_s[:, 2 * LRU_COLS * p:2 * LRU_COLS * (p + 1)] = _dot(xl_s[:, cols].astype(BF16), wai_ref[p])
        sp = LRU_C * _softplus(-lam_ref[:, cols])
        ba = ba_ref[:, cols]
        bx = bx_ref[:, cols]
        carry = hcar[:, cols]
        for r in range(t // ROW_STRIP):
            rows = slice(ROW_STRIP * r, ROW_STRIP * (r + 1))
            rg = jax.nn.sigmoid(ri_s[rows, rcols] + ba)
            ig = jax.nn.sigmoid(ri_s[rows, icols] + bx)
            nlog_a = rg * sp
            a = jnp.exp(-nlog_a)
            u = jnp.sqrt(jnp.tanh(nlog_a) * (a * a + 1.0)) * (ig * xl_s[rows, cols])
            hs = []
            for q in range(ROW_STRIP // SUBLANE):
                a8, u8 = _lru_group_scan(a[SUBLANE * q:SUBLANE * (q + 1)], u[SUBLANE * q:SUBLANE * (q + 1)])
                h8 = a8 * carry + u8
                carry = h8[SUBLANE - 1:SUBLANE, :]
                hs.append(h8)
            ycat[rows, cols] = (jnp.concatenate(hs, axis=0) * _gelu_tanh(gl_s[rows, cols])).astype(BF16)
        hcar[:, cols] = carry

    for q in range(SSD_CONV_DIM // SSD_COLS):
        cols = slice(SSD_COLS * q, SSD_COLS * (q + 1))
        ext_s[CONV_PAD:CONV_PAD + t, cols] = _dot(h_s[...], win_ref[:, s2 + SSD_COLS * q:s2 + SSD_COLS * (q + 1)])
        for r in range(t // CONV_STRIP):
            xbc_s[CONV_STRIP * r:CONV_STRIP * (r + 1), cols] = _silu(
                _conv_rows(ext_s, scw_ref, scb_ref, CONV_STRIP * r, CONV_STRIP, cols))
        ext_s[0:CONV_PAD, cols] = ext_s[t:t + CONV_PAD, cols]
    z_s[...] = _dot(h_s[...], win_ref[:, s1:s2])
    dt_s[...] = _softplus(_dot(h_s[...], wdt_ref[...]) + dtb_ref[...])

    ri_ = lax.broadcasted_iota(jnp.int32, (ln, ln), 0)
    ci_ = lax.broadcasted_iota(jnp.int32, (ln, ln), 1)
    causal = ci_ <= ri_
    tril = jnp.where(causal, 1.0, 0.0).astype(BF16)
    low_half = lax.broadcasted_iota(jnp.int32, (ln, LANE), 1) < SSD_HEADDIM
    for c in range(t // ln):
        _ssd_chunk(c, causal, tril, low_half, xbc_s, dt_s, z_s, sstate, ycat, alog_ref, dexp_ref, sng_ref, e_ref)

    y_mix = _dot(ycat[...], wout_ref[...])
    o_ref[0] = x_ref[0] + (1.0 + m[2:3]) * y_mix


def _hyb_call(x, mod3, g, w_in_b, w_out_b, e_idx, w_dt, lru_conv_w, lru_conv_b, lru_wa, lru_ba, lru_wx, lru_bx,
              lru_lambda, ssd_conv_w, ssd_conv_b, ssd_dt_bias, ssd_a_log, ssd_d, ssd_norm_g):
    bsz, seq, d = x.shape
    t = HYB_TILE
    s3 = HYB_SPLITS[-1]
    w_dt_p = jnp.zeros((d, LANE), BF16).at[:, :SSD_HEADS].set(w_dt.astype(BF16))
    npair = LRU_HEADS // 2
    za = jnp.zeros((npair, LRU_BLOCK, LRU_BLOCK), F32)
    wa2 = lru_wa.reshape(npair, 2, LRU_BLOCK, LRU_BLOCK)
    wx2 = lru_wx.reshape(npair, 2, LRU_BLOCK, LRU_BLOCK)
    wai = jnp.concatenate([jnp.concatenate([wa2[:, 0], za, wx2[:, 0], za], axis=-1),
                           jnp.concatenate([za, wa2[:, 1], za, wx2[:, 1]], axis=-1)], axis=1).astype(BF16)
    pad_h = lambda v: jnp.zeros((1, LANE), F32).at[0, :SSD_HEADS].set(v)
    expand = jnp.zeros((LANE, SSD_WIDTH), F32).at[:SSD_HEADS].set(
        jnp.repeat(jnp.eye(SSD_HEADS, dtype=F32), SSD_HEADDIM, axis=1)).astype(BF16)
    row = lambda v: v.reshape(1, -1)
    small = (row(g),)
    rest = (w_dt_p, lru_conv_w, row(lru_conv_b), wai, row(lru_ba), row(lru_bx), row(lru_lambda),
            ssd_conv_w, row(ssd_conv_b), pad_h(ssd_dt_bias), pad_h(ssd_a_log),
            row(jnp.repeat(ssd_d, SSD_HEADDIM)), row(ssd_norm_g), expand)
    in_specs = ([_tile_spec(t, d), _mod_spec(d), _const_spec((1, d)), _select_spec((d, s3), (e_idx,))]
                + [_const_spec(a.shape) for a in rest]
                + [_select_spec(w_out_b.shape[1:], (e_idx,))])
    scratch = [
        pltpu.VMEM((t, d), BF16),
        pltpu.VMEM((t + CONV_PAD, LRU_WIDTH), F32),
        pltpu.VMEM((t, LRU_WIDTH), F32),
        pltpu.VMEM((t, 2 * LRU_WIDTH), F32),
        pltpu.VMEM((t, LRU_WIDTH), F32),
        pltpu.VMEM((1, LRU_WIDTH), F32),
        pltpu.VMEM((t + CONV_PAD, SSD_CONV_DIM), F32),
        pltpu.VMEM((t, SSD_CONV_DIM), F32),
        pltpu.VMEM((t, LANE), F32),
        pltpu.VMEM((t, SSD_WIDTH), F32),
        pltpu.VMEM((SSD_STATE, SSD_WIDTH), F32),
        pltpu.VMEM((t, LRU_WIDTH + SSD_WIDTH), BF16),
    ]
    return pl.pallas_call(
        _hyb_kernel,
        grid=(bsz, seq // t),
        in_specs=in_specs,
        out_specs=_tile_spec(t, d),
        out_shape=jax.ShapeDtypeStruct((bsz, seq, d), F32),
        scratch_shapes=scratch,
        compiler_params=_params(("arbitrary", "arbitrary")),
        name="hybrid_mixer",
    )(x, mod3, *small, w_in_b, *rest, w_out_b)


def _mlstm_kernel(x_ref, mod_ref, g_ref, wup_ref, cw_ref, cb_ref, wqk_ref, wv_ref, wg_ref, bg_ref,
                  ng_ref, skip_ref, wdown_ref,
                  o_ref,
                  ext, q_s, k_s, v_s, cst, nst, mst, ycat):
    ln = x_ref.shape[1]
    w = MLSTM_WIDTH
    dh = MLSTM_HEADDIM
    nblk = w // MXU_DIM

    @pl.when(pl.program_id(1) == 0)
    def _():
        ext[0:CONV_PAD, :] = jnp.zeros((CONV_PAD, w), F32)
        cst[...] = jnp.zeros_like(cst)
        nst[...] = jnp.zeros_like(nst)
        mst[...] = jnp.zeros_like(mst)

    x = x_ref[0]
    m = mod_ref[0]
    h = _modulate(x, g_ref[...], m[0:1], m[1:2]).astype(BF16)
    xm = _dot(h, wup_ref[:, 0:w])
    z = _dot(h, wup_ref[:, w:2 * w])
    ext[CONV_PAD:CONV_PAD + ln, :] = xm
    xc = _silu(_conv_rows(ext, cw_ref, cb_ref, 0, ln, slice(0, w)))
    ext[0:CONV_PAD, :] = xm[ln - CONV_PAD:ln, :]
    xc_b = xc.astype(BF16)
    xm_b = xm.astype(BF16)
    for b in range(nblk):
        sl = slice(MXU_DIM * b, MXU_DIM * (b + 1))
        qk = _dot(xc_b[:, sl], wqk_ref[b])
        q_s[:, sl] = qk[:, :MXU_DIM]
        k_s[:, sl] = qk[:, MXU_DIM:]
        v_s[:, sl] = _dot(xm_b[:, sl], wv_ref[b])
    gates = (_dot(q_s[...].astype(BF16), wg_ref[0:w, :]) + _dot(k_s[...].astype(BF16), wg_ref[w:2 * w, :])
             + _dot(v_s[...].astype(BF16), wg_ref[2 * w:3 * w, :]) + bg_ref[...])
    log_f = -_softplus(-gates)
    ri_ = lax.broadcasted_iota(jnp.int32, (ln, ln), 0)
    ci_ = lax.broadcasted_iota(jnp.int32, (ln, ln), 1)
    causal = ci_ <= ri_
    tril = jnp.where(causal, 1.0, 0.0).astype(BF16)
    bcum = _dot_lhs01(tril, log_f, 3)
    gates_t = gates.T
    bcum_t = bcum.T
    k_scale = dh ** -0.5
    for hd in range(MLSTM_HEADS):
        sl = slice(dh * hd, dh * (hd + 1))
        fi = MLSTM_HEADS + hd
        bc_col = bcum[:, fi:fi + 1]
        bc_row = bcum_t[fi:fi + 1, :]
        i_col = gates[:, hd:hd + 1]
        i_row = gates_t[hd:hd + 1, :]
        m_prev = mst[hd][:, 0:1]
        log_d = jnp.where(causal, bc_col - bc_row + i_row, -jnp.inf)
        g_col = bc_col + m_prev
        m_new_rows = jnp.maximum(g_col, jnp.max(log_d, axis=-1, keepdims=True))
        w_inter = jnp.exp(g_col - m_new_rows)
        q_h = q_s[:, sl]
        k_h = k_s[:, sl] * k_scale
        q_b = q_h.astype(BF16)
        v_b = v_s[:, sl].astype(BF16)
        s_mat = lax.dot_general(q_b, k_h.astype(BF16), (((1,), (1,)), ((), ())),
                                preferred_element_type=F32) * jnp.exp(log_d - m_new_rows)
        c_prev = cst[hd]
        n_prev = nst[hd]
        num = w_inter * _dot(q_b, c_prev.astype(BF16)) + _dot(s_mat.astype(BF16), v_b)
        den = (w_inter * jnp.sum(q_h * n_prev, axis=-1, keepdims=True)
               + jnp.sum(s_mat, axis=-1, keepdims=True))
        h_out = num / jnp.maximum(jnp.abs(den), jnp.exp(-m_new_rows))
        b_last = bc_col[ln - 1:ln, :]
        log_w = b_last - bc_col + i_col
        m_next = jnp.maximum(b_last + m_prev, jnp.max(log_w, axis=0, keepdims=True))
        w_s = jnp.exp(log_w - m_next)
        decay = jnp.exp(b_last + m_prev - m_next)
        kw = k_h * w_s
        cst[hd] = decay * c_prev + lax.dot_general(kw.astype(BF16), v_b, (((0,), (0,)), ((), ())),
                                                   preferred_element_type=F32)
        nst[hd] = decay * n_prev + jnp.sum(kw, axis=0, keepdims=True)
        mst[hd] = jnp.broadcast_to(m_next, (1, LANE))
        mu = jnp.mean(h_out, axis=-1, keepdims=True)
        hc = h_out - mu
        var = jnp.mean(hc * hc, axis=-1, keepdims=True)
        hn = (hc * lax.rsqrt(var + EPS)) * ng_ref[:, sl]
        out = (hn + skip_ref[:, sl] * xc[:, sl]) * _silu(z[:, sl])
        ycat[:, sl] = out.astype(BF16)

    y_mix = _dot(ycat[...], wdown_ref[...])
    o_ref[0] = x + (1.0 + m[2:3]) * y_mix


def _expand_block_diag(w, group):
    nb, bi, bo = w.shape
    wg = w.reshape(nb // group, group, bi, bo)
    eye = jnp.eye(group, dtype=w.dtype)
    return (wg[:, :, :, None, :] * eye[None, :, None, :, None]).reshape(nb // group, group * bi, group * bo)


def _mlstm_call(x, mod3, g, w_up_b, w_down_b, o_idx, conv_w, conv_b, wq, wk, wv, w_gates, b_gates, norm_g, skip):
    bsz, seq, d = x.shape
    ln = MLSTM_CHUNK
    w = MLSTM_WIDTH
    grp = MXU_DIM // MLSTM_QKV_BLOCK
    wqk = jnp.concatenate([_expand_block_diag(wq, grp), _expand_block_diag(wk, grp)], axis=-1).astype(BF16)
    wvd = _expand_block_diag(wv, grp).astype(BF16)
    wg_p = jnp.zeros((3 * w, LANE), BF16).at[:, :2 * MLSTM_HEADS].set(w_gates.astype(BF16))
    bg_p = jnp.zeros((1, LANE), F32).at[0, :2 * MLSTM_HEADS].set(b_gates)
    row = lambda v: v.reshape(1, -1)
    mid = (conv_w, row(conv_b), wqk, wvd, wg_p, bg_p, row(norm_g), row(skip))
    in_specs = ([_tile_spec(ln, d), _mod_spec(d), _const_spec((1, d)), _select_spec(w_up_b.shape[1:], (o_idx,))]
                + [_const_spec(a.shape) for a in mid]
                + [_select_spec(w_down_b.shape[1:], (o_idx,))])
    scratch = [
        pltpu.VMEM((ln + CONV_PAD, w), F32),
        pltpu.VMEM((ln, w), F32),
        pltpu.VMEM((ln, w), F32),
        pltpu.VMEM((ln, w), F32),
        pltpu.VMEM((MLSTM_HEADS, MLSTM_HEADDIM, MLSTM_HEADDIM), F32),
        pltpu.VMEM((MLSTM_HEADS, 1, MLSTM_HEADDIM), F32),
        pltpu.VMEM((MLSTM_HEADS, 1, LANE), F32),
        pltpu.VMEM((ln, w), BF16),
    ]
    return pl.pallas_call(
        _mlstm_kernel,
        grid=(bsz, seq // ln),
        in_specs=in_specs,
        out_specs=_tile_spec(ln, d),
        out_shape=jax.ShapeDtypeStruct((bsz, seq, d), F32),
        scratch_shapes=scratch,
        compiler_params=_params(("arbitrary", "arbitrary")),
        name="mlstm_mixer",
    )(x, mod3, row(g), w_up_b, *mid, w_down_b)


def kernel(x, c, ada_w, ada_b, norm_g, ffn_w_gate, ffn_w_up, ffn_w_down, hyb_w_in, hyb_w_out, lru_conv_w, lru_conv_b, lru_wa, lru_ba, lru_wx, lru_bx, lru_lambda, ssd_conv_w, ssd_conv_b, ssd_dt_bias, ssd_a_log, ssd_d, ssd_norm_g, mlstm_w_up, mlstm_conv_w, mlstm_conv_b, mlstm_wq, mlstm_wk, mlstm_wv, mlstm_w_gates, mlstm_b_gates, mlstm_norm_g, mlstm_skip, mlstm_w_down, final_norm_g):
    depth = ada_w.shape[0]
    mod = _ada_call(c, ada_w, ada_b)
    wg_b, wu_b, wd_b = _cast_bf16(ffn_w_gate), _cast_bf16(ffn_w_up), _cast_bf16(ffn_w_down)
    hyb_in_b, hyb_out_b = _cast_bf16(hyb_w_in), _cast_bf16(hyb_w_out)
    up_b, down_b = _cast_bf16(mlstm_w_up), _cast_bf16(mlstm_w_down)
    for layer in range(depth):
        x = _ffn_call(x, mod[layer, :, 0], norm_g[layer, 0], wg_b, wu_b, wd_b, final_norm_g,
                      lead=(layer, 0), final_norm=False)
        if layer % 2 == 0:
            e = layer // 2
            x = _hyb_call(x, mod[layer, :, 1], norm_g[layer, 1], hyb_in_b, hyb_out_b, e,
                          hyb_w_in[e][:, HYB_SPLITS[-1]:], lru_conv_w[e], lru_conv_b[e], lru_wa[e], lru_ba[e],
                          lru_wx[e], lru_bx[e], lru_lambda[e], ssd_conv_w[e], ssd_conv_b[e], ssd_dt_bias[e],
                          ssd_a_log[e], ssd_d[e], ssd_norm_g[e])
        else:
            o = layer // 2
            x = _mlstm_call(x, mod[layer, :, 1], norm_g[layer, 1], up_b, down_b, o, mlstm_conv_w[o], mlstm_conv_b[o],
                            mlstm_wq[o], mlstm_wk[o], mlstm_wv[o], mlstm_w_gates[o], mlstm_b_gates[o],
                            mlstm_norm_g[o], mlstm_skip[o])
        x = _ffn_call(x, mod[layer, :, 2], norm_g[layer, 2], wg_b, wu_b, wd_b, final_norm_g,
                      lead=(layer, 1), final_norm=(layer == depth - 1))
    return x
```

```python
import functools
import math

import jax
import jax.numpy as jnp
from jax import lax
from jax.experimental import pallas as pl
from jax.experimental.pallas import tpu as pltpu

F32 = jnp.float32
BF16 = jnp.bfloat16

D_MODEL = 1024
N_SUB = 3
HALF_STEP = 0.5
D_FF = 2816
CONV_WIDTH = 4
EPS = 1e-6
LRU_WIDTH = D_MODEL
LRU_HEADS = 8
LRU_BLOCK = LRU_WIDTH // LRU_HEADS
LRU_C = 8.0
SSD_WIDTH = D_MODEL
SSD_HEADDIM = 64
SSD_HEADS = SSD_WIDTH // SSD_HEADDIM
SSD_GROUPS = 2
SSD_STATE = 128
SSD_CHUNK = 128
SSD_CONV_DIM = SSD_WIDTH + 2 * SSD_GROUPS * SSD_STATE
HYB_SPLITS = (LRU_WIDTH, 2 * LRU_WIDTH, 2 * LRU_WIDTH + SSD_WIDTH, 2 * LRU_WIDTH + SSD_WIDTH + SSD_CONV_DIM)
MLSTM_WIDTH = 2 * D_MODEL
MLSTM_HEADS = 4
MLSTM_HEADDIM = MLSTM_WIDTH // MLSTM_HEADS
MLSTM_QKV_BLOCK = 4

LANE = 128
SUBLANE = 8
PACKED_ROWS = 16
MXU_DIM = 256
VMEM_LIMIT_BYTES = 56 * 1024 * 1024

FFN_TILE = 512
FFN_PART = 256
HYB_TILE = 512
MLSTM_CHUNK = 256
ADA_TILE_N = 1536
CAST_ROWS = 512
CONV_PAD = SUBLANE
ROW_STRIP = PACKED_ROWS
CONV_STRIP = 64
LRU_COLS = 2 * LRU_BLOCK
SSD_COLS = 512


def _silu(x):
    return x * jax.nn.sigmoid(x)


def _softplus(x):
    return jnp.maximum(x, 0.0) + jnp.log1p(jnp.exp(-jnp.abs(x)))


def _gelu_tanh(x):
    c = math.sqrt(2.0 / math.pi)
    return x * (0.5 * (1.0 + jnp.tanh(c * (x + 0.044715 * (x * x * x)))))


def _modulate(x, g, shift, scale):
    y = x * lax.rsqrt(jnp.mean(x * x, axis=-1, keepdims=True) + EPS)
    return (y * g) * (1.0 + scale) + shift


def _modulate_rows(x_ref, h_ref, g, shift, scale, t):
    for r in range(t // ROW_STRIP):
        rows = slice(ROW_STRIP * r, ROW_STRIP * (r + 1))
        h_ref[rows, :] = _modulate(x_ref[0, rows, :], g, shift, scale).astype(BF16)


def _split_bf16(x, n):
    parts = []
    r = x
    for _ in range(n):
        p = r.astype(BF16)
        parts.append(p)
        r = r - p.astype(F32)
    return parts


def _dot(a, b):
    return jnp.dot(a, b, preferred_element_type=F32)


def _dot_lhs01(m01, x, n):
    out = None
    for p in _split_bf16(x, n):
        t = _dot(m01, p)
        out = t if out is None else out + t
    return out


def _dot_rhs01(x, m01, n):
    out = None
    for p in _split_bf16(x, n):
        t = _dot(p, m01)
        out = t if out is None else out + t
    return out


def _conv_rows(ext_ref, w_ref, b_ref, r0, nrows, cols):
    ntile = nrows // SUBLANE
    tiles = [ext_ref[r0 + SUBLANE * i:r0 + SUBLANE * (i + 1), cols] for i in range(ntile + 1)]
    row = lax.broadcasted_iota(jnp.int32, tiles[0].shape, 0)
    bias = b_ref[:, cols]
    w_now = w_ref[CONV_WIDTH - 1:CONV_WIDTH, cols]
    acc = [bias + w_now * tiles[i + 1] for i in range(ntile)]
    for s in range(1, CONV_WIDTH):
        w_s = w_ref[CONV_WIDTH - 1 - s:CONV_WIDTH - s, cols]
        rot = [pltpu.roll(tl, s, 0) for tl in tiles]
        head = row < s
        acc = [acc[i] + w_s * jnp.where(head, rot[i], rot[i + 1]) for i in range(ntile)]
    return jnp.concatenate(acc, axis=0)


def _const_spec(shape):
    nd = len(shape)
    return pl.BlockSpec(shape, lambda *_: (0,) * nd, pipeline_mode=pl.Buffered(1))


def _select_spec(block, lead):
    nl = len(lead)
    return pl.BlockSpec((None,) * nl + tuple(block), lambda *_: tuple(lead) + (0,) * len(block),
                        pipeline_mode=pl.Buffered(1))


def _tile_spec(t, d):
    return pl.BlockSpec((1, t, d), lambda b, i: (b, i, 0))


def _mod_spec(d):
    return pl.BlockSpec((1, 3, d), lambda b, i: (b, 0, 0))


def _params(semantics):
    return pltpu.CompilerParams(dimension_semantics=semantics, vmem_limit_bytes=VMEM_LIMIT_BYTES)


def _cast_kernel(w_ref, o_ref):
    o_ref[...] = w_ref[...].astype(BF16)


def _cast_bf16(w):
    shape = w.shape
    cols = shape[-1]
    rows = math.prod(shape[:-1])
    out = pl.pallas_call(
        _cast_kernel,
        grid=(rows // CAST_ROWS,),
        in_specs=[pl.BlockSpec((CAST_ROWS, cols), lambda i: (i, 0))],
        out_specs=pl.BlockSpec((CAST_ROWS, cols), lambda i: (i, 0)),
        out_shape=jax.ShapeDtypeStruct((rows, cols), BF16),
        compiler_params=_params(("arbitrary",)),
        name="cast_bf16",
    )(w.reshape(rows, cols))
    return out.reshape(shape)


def _ada_kernel(c_ref, w_ref, b_ref, o_ref):
    c = c_ref[...]
    ca = _silu(c)
    w = w_ref[0]
    c1, c2 = _split_bf16(ca, 2)
    w1, w2 = _split_bf16(w, 2)
    o_ref[0] = _dot(c1, w1) + (_dot(c1, w2) + _dot(c2, w1)) + b_ref[0]


def _ada_call(c, ada_w, ada_b):
    depth, d, n = ada_w.shape
    bsz = c.shape[0]
    rows = -(-bsz // SUBLANE) * SUBLANE
    c_pad = jnp.zeros((rows, d), F32).at[:bsz].set(c)
    out = pl.pallas_call(
        _ada_kernel,
        grid=(depth, n // ADA_TILE_N),
        in_specs=[pl.BlockSpec((rows, d), lambda l, j: (0, 0)),
                  pl.BlockSpec((1, d, ADA_TILE_N), lambda l, j: (l, 0, j)),
                  pl.BlockSpec((1, 1, ADA_TILE_N), lambda l, j: (l, 0, j))],
        out_specs=pl.BlockSpec((1, rows, ADA_TILE_N), lambda l, j: (l, 0, j)),
        out_shape=jax.ShapeDtypeStruct((depth, rows, n), F32),
        compiler_params=_params(("arbitrary", "arbitrary")),
        name="ada_mod",
    )(c_pad, ada_w, ada_b.reshape(depth, 1, n))
    return out[:, :bsz].reshape(depth, bsz, N_SUB, 3, d)


def _ffn_kernel(x_ref, mod_ref, g_ref, wg_ref, wu_ref, wd_ref, fg_ref, o_ref, *, final_norm):
    m = mod_ref[0]
    t = x_ref.shape[1]
    parts = [slice(FFN_PART * p, FFN_PART * (p + 1)) for p in range(t // FFN_PART)]
    xs = [x_ref[0, rows, :] for rows in parts]
    hs = [_modulate(x, g_ref[...], m[0:1], m[1:2]).astype(BF16) for x in xs]
    gates = [_dot(h, wg_ref[...]) for h in hs]
    ups = [_dot(h, wu_ref[...]) for h in hs]
    acts = [(_silu(gate) * up).astype(BF16) for gate, up in zip(gates, ups)]
    ys = [_dot(act, wd_ref[...]) for act in acts]
    for rows, x, y in zip(parts, xs, ys):
        o = x + (HALF_STEP * (1.0 + m[2:3])) * y
        if final_norm:
            o = (o * lax.rsqrt(jnp.mean(o * o, axis=-1, keepdims=True) + EPS)) * fg_ref[...]
        o_ref[0, rows, :] = o


def _ffn_call(x, mod3, g, wg, wu, wd, final_g, *, lead, final_norm):
    bsz, seq, d = x.shape
    f = wg.shape[-1]
    t = FFN_TILE
    return pl.pallas_call(
        functools.partial(_ffn_kernel, final_norm=final_norm),
        grid=(bsz, seq // t),
        in_specs=[_tile_spec(t, d), _mod_spec(d), _const_spec((1, d)),
                  _select_spec((d, f), lead), _select_spec((d, f), lead), _select_spec((f, d), lead),
                  _const_spec((1, d))],
        out_specs=_tile_spec(t, d),
        out_shape=jax.ShapeDtypeStruct((bsz, seq, d), F32),
        compiler_params=_params(("arbitrary", "arbitrary")),
        name="ffn",
    )(x, mod3, g.reshape(1, d), wg, wu, wd, final_g.reshape(1, d))


def _lru_group_scan(a, u):
    row = lax.broadcasted_iota(jnp.int32, a.shape, 0)
    for d in (1, 2, 4):
        a_prev = pltpu.roll(a, d, 0)
        u_prev = pltpu.roll(u, d, 0)
        keep = row >= d
        u = jnp.where(keep, a * u_prev + u, u)
        a = jnp.where(keep, a * a_prev, a)
    return a, u


def _ssd_chunk(c, causal, tril, low_half, xbc_s, dt_s, z_s, sstate, ycat, alog_ref, dexp_ref, sng_ref, e_ref):
    ln = SSD_CHUNK
    gn = SSD_GROUPS * SSD_STATE
    gw = SSD_WIDTH // SSD_GROUPS
    hpg = SSD_HEADS // SSD_GROUPS
    rows = slice(c * ln, (c + 1) * ln)
    xs = xbc_s[rows, 0:SSD_WIDTH]
    bm = xbc_s[rows, SSD_WIDTH:SSD_WIDTH + gn]
    cm = xbc_s[rows, SSD_WIDTH + gn:SSD_WIDTH + 2 * gn]
    dt = dt_s[rows, :]
    ac = dt * (-jnp.exp(alog_ref[...]))
    acs = _dot_lhs01(tril, ac, 3)
    acs_last = acs[ln - 1:ln, :]
    acs_t = acs.T
    dt_t = dt.T
    e = e_ref[...]
    dec_out = _dot_rhs01(jnp.exp(acs), e, 1)
    w_st = _dot_rhs01(dt * jnp.exp(acs_last - acs), e, 1)
    d_a = _dot_rhs01(jnp.broadcast_to(jnp.exp(acs_last), (SUBLANE, LANE)), e, 2)[0:1]
    xw_b = (xs * w_st).astype(BF16)
    s_prev = sstate[...]
    s_prev_b = s_prev.astype(BF16)
    ys = []
    sts = []
    for g in range(SSD_GROUPS):
        bg = bm[:, SSD_STATE * g:SSD_STATE * (g + 1)]
        cg_b = cm[:, SSD_STATE * g:SSD_STATE * (g + 1)].astype(BF16)
        cb = lax.dot_general(cg_b, bg.astype(BF16), (((1,), (1,)), ((), ())),
                             preferred_element_type=F32)
        y_off = _dot(cg_b, s_prev_b[:, gw * g:gw * (g + 1)])
        sts.append(_dot(bg.T.astype(BF16), xw_b[:, gw * g:gw * (g + 1)]))
        for j in range(hpg // 2):
            pair = (hpg // 2) * g + j
            mats = []
            for hd in (2 * pair, 2 * pair + 1):
                diff = jnp.where(causal, acs[:, hd:hd + 1] - acs_t[hd:hd + 1, :], -jnp.inf)
                mats.append(cb * jnp.exp(diff) * dt_t[hd:hd + 1, :])
            lhs = jnp.concatenate(mats, axis=1).astype(BF16)
            xp = xs[:, LANE * pair:LANE * (pair + 1)]
            rhs = jnp.concatenate([jnp.where(low_half, xp, 0.0), jnp.where(low_half, 0.0, xp)],
                                  axis=0).astype(BF16)
            ys.append(_dot(lhs, rhs)
                      + y_off[:, LANE * j:LANE * (j + 1)] * dec_out[:, LANE * pair:LANE * (pair + 1)])
    sstate[...] = s_prev * d_a + jnp.concatenate(sts, axis=1)
    y = jnp.concatenate(ys, axis=1) + dexp_ref[...] * xs
    y = y * _silu(z_s[rows, :])
    for g in range(SSD_GROUPS):
        yg = y[:, gw * g:gw * (g + 1)]
        yn = (yg * lax.rsqrt(jnp.mean(yg * yg, axis=-1, keepdims=True) + EPS)) * sng_ref[:, gw * g:gw * (g + 1)]
        ycat[rows, LRU_WIDTH + gw * g:LRU_WIDTH + gw * (g + 1)] = yn.astype(BF16)


def _hyb_kernel(x_ref, mod_ref, g_ref, win_ref, wdt_ref, lcw_ref, lcb_ref, wai_ref, ba_ref, bx_ref, lam_ref,
                scw_ref, scb_ref, dtb_ref, alog_ref, dexp_ref, sng_ref, e_ref, wout_ref,
                o_ref,
                h_s, ext_l, xl_s, ri_s, gl_s, hcar, ext_s, xbc_s, dt_s, z_s, sstate, ycat):
    t = x_ref.shape[1]
    ln = SSD_CHUNK
    s0, s1, s2, s3 = HYB_SPLITS

    @pl.when(pl.program_id(1) == 0)
    def _():
        ext_l[0:CONV_PAD, :] = jnp.zeros((CONV_PAD, LRU_WIDTH), F32)
        ext_s[0:CONV_PAD, :] = jnp.zeros((CONV_PAD, SSD_CONV_DIM), F32)
        hcar[...] = jnp.zeros_like(hcar)
        sstate[...] = jnp.zeros_like(sstate)

    m = mod_ref[0]
    _modulate_rows(x_ref, h_s, g_ref[...], m[0:1], m[1:2], t)

    n_lru = LRU_WIDTH // LRU_COLS
    n_ssd = SSD_CONV_DIM // SSD_COLS

    def project_lru(p):
        cols = slice(LRU_COLS * p, LRU_COLS * (p + 1))
        ext_l[CONV_PAD:CONV_PAD + t, cols] = _dot(h_s[...], win_ref[:, s0 + LRU_COLS * p:s0 + LRU_COLS * (p + 1)])
        gl_s[:, cols] = _dot(h_s[...], win_ref[:, LRU_COLS * p:LRU_COLS * (p + 1)])

    def project_ssd(q):
        cols = slice(SSD_COLS * q, SSD_COLS * (q + 1))
        ext_s[CONV_PAD:CONV_PAD + t, cols] = _dot(h_s[...], win_ref[:, s2 + SSD_COLS * q:s2 + SSD_COLS * (q + 1)])

    project_lru(0)
    for p in range(n_lru):
        cols = slice(LRU_COLS * p, LRU_COLS * (p + 1))
        rcols = slice(2 * LRU_COLS * p, 2 * LRU_COLS * p + LRU_COLS)
        icols = slice(2 * LRU_COLS * p + LRU_COLS, 2 * LRU_COLS * (p + 1))
        if p + 1 < n_lru:
            project_lru(p + 1)
        else:
            project_ssd(0)
        for r in range(t // CONV_STRIP):
            xl_s[CONV_STRIP * r:CONV_STRIP * (r + 1), cols] = _conv_rows(ext_l, lcw_ref, lcb_ref, CONV_STRIP * r,
                                                                          CONV_STRIP, cols)
        ext_l[0:CONV_PAD, cols] = ext_l[t:t + CONV_PAD, cols]
        ri_s[:, 2 * LRU_COLS * p:2 * LRU_COLS * (p + 1)] = _dot(xl_s[:, cols].astype(BF16), wai_ref[p])
        sp = LRU_C * _softplus(-lam_ref[:, cols])
        ba = ba_ref[:, cols]
        bx = bx_ref[:, cols]
        carry = hcar[:, cols]
        for r in range(t // ROW_STRIP):
            rows = slice(ROW_STRIP * r, ROW_STRIP * (r + 1))
            rg = jax.nn.sigmoid(ri_s[rows, rcols] + ba)
            ig = jax.nn.sigmoid(ri_s[rows, icols] + bx)
            nlog_a = rg * sp
            a = jnp.exp(-nlog_a)
            u = jnp.sqrt(jnp.tanh(nlog_a) * (a * a + 1.0)) * (ig * xl_s[rows, cols])
            hs = []
            for q in range(ROW_STRIP // SUBLANE):
                a8, u8 = _lru_group_scan(a[SUBLANE * q:SUBLANE * (q + 1)], u[SUBLANE * q:SUBLANE * (q + 1)])
                h8 = a8 * carry + u8
                carry = h8[SUBLANE - 1:SUBLANE, :]
                hs.append(h8)
            ycat[rows, cols] = (jnp.concatenate(hs, axis=0) * _gelu_tanh(gl_s[rows, cols])).astype(BF16)
        hcar[:, cols] = carry

    for q in range(n_ssd):
        cols = slice(SSD_COLS * q, SSD_COLS * (q + 1))
        if q + 1 < n_ssd:
            project_ssd(q + 1)
        else:
            z_s[...] = _dot(h_s[...], win_ref[:, s1:s2])
            dt_s[...] = _softplus(_dot(h_s[...], wdt_ref[...]) + dtb_ref[...])
        for r in range(t // CONV_STRIP):
            xbc_s[CONV_STRIP * r:CONV_STRIP * (r + 1), cols] = _silu(
                _conv_rows(ext_s, scw_ref, scb_ref, CONV_STRIP * r, CONV_STRIP, cols))
        ext_s[0:CONV_PAD, cols] = ext_s[t:t + CONV_PAD, cols]
    y_lru = _dot(ycat[:, 0:LRU_WIDTH], wout_ref[0:LRU_WIDTH, :])

    ri_ = lax.broadcasted_iota(jnp.int32, (ln, ln), 0)
    ci_ = lax.broadcasted_iota(jnp.int32, (ln, ln), 1)
    causal = ci_ <= ri_
    tril = jnp.where(causal, 1.0, 0.0).astype(BF16)
    low_half = lax.broadcasted_iota(jnp.int32, (ln, LANE), 1) < SSD_HEADDIM
    for c in range(t // ln):
        _ssd_chunk(c, causal, tril, low_half, xbc_s, dt_s, z_s, sstate, ycat, alog_ref, dexp_ref, sng_ref, e_ref)

    y_mix = y_lru + _dot(ycat[:, LRU_WIDTH:], wout_ref[LRU_WIDTH:, :])
    o_ref[0] = x_ref[0] + (1.0 + m[2:3]) * y_mix


def _hyb_call(x, mod3, g, w_in_b, w_out_b, e_idx, w_dt, lru_conv_w, lru_conv_b, lru_wa, lru_ba, lru_wx, lru_bx,
              lru_lambda, ssd_conv_w, ssd_conv_b, ssd_dt_bias, ssd_a_log, ssd_d, ssd_norm_g):
    bsz, seq, d = x.shape
    t = HYB_TILE
    s3 = HYB_SPLITS[-1]
    w_dt_p = jnp.zeros((d, LANE), BF16).at[:, :SSD_HEADS].set(w_dt.astype(BF16))
    npair = LRU_HEADS // 2
    za = jnp.zeros((npair, LRU_BLOCK, LRU_BLOCK), F32)
    wa2 = lru_wa.reshape(npair, 2, LRU_BLOCK, LRU_BLOCK)
    wx2 = lru_wx.reshape(npair, 2, LRU_BLOCK, LRU_BLOCK)
    wai = jnp.concatenate([jnp.concatenate([wa2[:, 0], za, wx2[:, 0], za], axis=-1),
                           jnp.concatenate([za, wa2[:, 1], za, wx2[:, 1]], axis=-1)], axis=1).astype(BF16)
    pad_h = lambda v: jnp.zeros((1, LANE), F32).at[0, :SSD_HEADS].set(v)
    expand = jnp.zeros((LANE, SSD_WIDTH), F32).at[:SSD_HEADS].set(
        jnp.repeat(jnp.eye(SSD_HEADS, dtype=F32), SSD_HEADDIM, axis=1)).astype(BF16)
    row = lambda v: v.reshape(1, -1)
    small = (row(g),)
    rest = (w_dt_p, lru_conv_w, row(lru_conv_b), wai, row(lru_ba), row(lru_bx), row(lru_lambda),
            ssd_conv_w, row(ssd_conv_b), pad_h(ssd_dt_bias), pad_h(ssd_a_log),
            row(jnp.repeat(ssd_d, SSD_HEADDIM)), row(ssd_norm_g), expand)
    in_specs = ([_tile_spec(t, d), _mod_spec(d), _const_spec((1, d)), _select_spec((d, s3), (e_idx,))]
                + [_const_spec(a.shape) for a in rest]
                + [_select_spec(w_out_b.shape[1:], (e_idx,))])
    scratch = [
        pltpu.VMEM((t, d), BF16),
        pltpu.VMEM((t + CONV_PAD, LRU_WIDTH), F32),
        pltpu.VMEM((t, LRU_WIDTH), F32),
        pltpu.VMEM((t, 2 * LRU_WIDTH), F32),
        pltpu.VMEM((t, LRU_WIDTH), F32),
        pltpu.VMEM((1, LRU_WIDTH), F32),
        pltpu.VMEM((t + CONV_PAD, SSD_CONV_DIM), F32),
        pltpu.VMEM((t, SSD_CONV_DIM), F32),
        pltpu.VMEM((t, LANE), F32),
        pltpu.VMEM((t, SSD_WIDTH), F32),
        pltpu.VMEM((SSD_STATE, SSD_WIDTH), F32),
        pltpu.VMEM((t, LRU_WIDTH + SSD_WIDTH), BF16),
    ]
    return pl.pallas_call(
        _hyb_kernel,
        grid=(bsz, seq // t),
        in_specs=in_specs,
        out_specs=_tile_spec(t, d),
        out_shape=jax.ShapeDtypeStruct((bsz, seq, d), F32),
        scratch_shapes=scratch,
        compiler_params=_params(("arbitrary", "arbitrary")),
        name="hybrid_mixer",
    )(x, mod3, *small, w_in_b, *rest, w_out_b)


def _mlstm_kernel(x_ref, mod_ref, g_ref, wup_ref, cw_ref, cb_ref, wqk_ref, wv_ref, wg_ref, bg_ref,
                  ng_ref, skip_ref, wdown_ref,
                  o_ref,
                  ext, q_s, k_s, v_s, cst, cst_b, nst, mst):
    ln = x_ref.shape[1]
    w = MLSTM_WIDTH
    dh = MLSTM_HEADDIM
    nblk = w // MXU_DIM

    @pl.when(pl.program_id(1) == 0)
    def _():
        ext[0:CONV_PAD, :] = jnp.zeros((CONV_PAD, w), F32)
        cst[...] = jnp.zeros_like(cst)
        cst_b[...] = jnp.zeros_like(cst_b)
        nst[...] = jnp.zeros_like(nst)
        mst[...] = jnp.zeros_like(mst)

    x = x_ref[0]
    m = mod_ref[0]
    h = _modulate(x, g_ref[...], m[0:1], m[1:2]).astype(BF16)
    xm = _dot(h, wup_ref[:, 0:w])
    z = _dot(h, wup_ref[:, w:2 * w])
    ext[CONV_PAD:CONV_PAD + ln, :] = xm
    xc = _silu(_conv_rows(ext, cw_ref, cb_ref, 0, ln, slice(0, w)))
    ext[0:CONV_PAD, :] = xm[ln - CONV_PAD:ln, :]
    xc_b = xc.astype(BF16)
    xm_b = xm.astype(BF16)
    for b in range(nblk):
        sl = slice(MXU_DIM * b, MXU_DIM * (b + 1))
        qk = _dot(xc_b[:, sl], wqk_ref[b])
        q_s[:, sl] = qk[:, :MXU_DIM]
        k_s[:, sl] = qk[:, MXU_DIM:]
        v_s[:, sl] = _dot(xm_b[:, sl], wv_ref[b])
    gates = (_dot(q_s[...].astype(BF16), wg_ref[0:w, :]) + _dot(k_s[...].astype(BF16), wg_ref[w:2 * w, :])
             + _dot(v_s[...].astype(BF16), wg_ref[2 * w:3 * w, :]) + bg_ref[...])
    log_f = -_softplus(-gates)
    ri_ = lax.broadcasted_iota(jnp.int32, (ln, ln), 0)
    ci_ = lax.broadcasted_iota(jnp.int32, (ln, ln), 1)
    causal = ci_ <= ri_
    tril = jnp.where(causal, 1.0, 0.0).astype(BF16)
    bcum = _dot_lhs01(tril, log_f, 3)
    gates_t = gates.T
    bcum_t = bcum.T
    k_scale = dh ** -0.5
    heads = range(MLSTM_HEADS)
    hsl = [slice(dh * hd, dh * (hd + 1)) for hd in heads]
    st = []
    for hd in heads:
        fi = MLSTM_HEADS + hd
        bc_col = bcum[:, fi:fi + 1]
        bc_row = bcum_t[fi:fi + 1, :]
        i_col = gates[:, hd:hd + 1]
        i_row = gates_t[hd:hd + 1, :]
        m_prev = mst[hd][:, 0:1]
        log_d = jnp.where(causal, bc_col - bc_row + i_row, -jnp.inf)
        g_col = bc_col + m_prev
        m_rows = jnp.maximum(g_col, jnp.max(log_d, axis=-1, keepdims=True))
        b_last = bc_col[ln - 1:ln, :]
        log_w = b_last - bc_col + i_col
        m_next = jnp.maximum(b_last + m_prev, jnp.max(log_w, axis=0, keepdims=True))
        st.append(dict(m_rows=m_rows, w_inter=jnp.exp(g_col - m_rows), d_mat=jnp.exp(log_d - m_rows),
                       w_s=jnp.exp(log_w - m_next), decay=jnp.exp(b_last + m_prev - m_next), m_next=m_next))
    for hd in heads:
        s = st[hd]
        q_h = q_s[:, hsl[hd]]
        k_h = k_s[:, hsl[hd]] * k_scale
        q_b = q_h.astype(BF16)
        v_b = v_s[:, hsl[hd]].astype(BF16)
        kw = k_h * s["w_s"]
        s.update(q_h=q_h, v_b=v_b, kw_sum=jnp.sum(kw, axis=0, keepdims=True),
                 qk=lax.dot_general(q_b, k_h.astype(BF16), (((1,), (1,)), ((), ())), preferred_element_type=F32),
                 qc=_dot(q_b, cst_b[hd]),
                 upd=lax.dot_general(kw.astype(BF16), v_b, (((0,), (0,)), ((), ())), preferred_element_type=F32))
    for hd in heads:
        s = st[hd]
        s_mat = s["qk"] * s["d_mat"]
        n_prev = nst[hd]
        num = s["w_inter"] * s["qc"] + _dot(s_mat.astype(BF16), s["v_b"])
        den = (s["w_inter"] * jnp.sum(s["q_h"] * n_prev, axis=-1, keepdims=True)
               + jnp.sum(s_mat, axis=-1, keepdims=True))
        s["h_out"] = num / jnp.maximum(jnp.abs(den), jnp.exp(-s["m_rows"]))
        c_new = s["decay"] * cst[hd] + s["upd"]
        cst[hd] = c_new
        cst_b[hd] = c_new.astype(BF16)
        nst[hd] = s["decay"] * n_prev + s["kw_sum"]
        mst[hd] = jnp.broadcast_to(s["m_next"], (1, LANE))
    y_mix = None
    for hd in heads:
        sl = hsl[hd]
        h_out = st[hd]["h_out"]
        mu = jnp.mean(h_out, axis=-1, keepdims=True)
        hc = h_out - mu
        var = jnp.mean(hc * hc, axis=-1, keepdims=True)
        hn = (hc * lax.rsqrt(var + EPS)) * ng_ref[:, sl]
        out = (hn + skip_ref[:, sl] * xc[:, sl]) * _silu(z[:, sl])
        y_head = _dot(out.astype(BF16), wdown_ref[sl, :])
        y_mix = y_head if y_mix is None else y_mix + y_head

    o_ref[0] = x + (1.0 + m[2:3]) * y_mix


def _expand_block_diag(w, group):
    nb, bi, bo = w.shape
    rows = w.reshape(nb // group, group * bi, bo)
    r_idx = jnp.arange(group * bi)[:, None]
    c_idx = jnp.arange(group * bo)[None, :]
    period = (c_idx % bo == jnp.arange(bo)[:, None]).astype(w.dtype)
    tiled = jnp.einsum('gro,oc->grc', rows, period, precision=lax.Precision.HIGHEST)
    return jnp.where((r_idx // bi == c_idx // bo)[None], tiled, 0.0)


def _mlstm_call(x, mod3, g, w_up_b, w_down_b, o_idx, conv_w, conv_b, wq, wk, wv, w_gates, b_gates, norm_g, skip):
    bsz, seq, d = x.shape
    ln = MLSTM_CHUNK
    w = MLSTM_WIDTH
    grp = MXU_DIM // MLSTM_QKV_BLOCK
    wqk = jnp.concatenate([_expand_block_diag(wq, grp), _expand_block_diag(wk, grp)], axis=-1).astype(BF16)
    wvd = _expand_block_diag(wv, grp).astype(BF16)
    wg_p = jnp.zeros((3 * w, LANE), BF16).at[:, :2 * MLSTM_HEADS].set(w_gates.astype(BF16))
    bg_p = jnp.zeros((1, LANE), F32).at[0, :2 * MLSTM_HEADS].set(b_gates)
    row = lambda v: v.reshape(1, -1)
    mid = (conv_w, row(conv_b), wqk, wvd, wg_p, bg_p, row(norm_g), row(skip))
    in_specs = ([_tile_spec(ln, d), _mod_spec(d), _const_spec((1, d)), _select_spec(w_up_b.shape[1:], (o_idx,))]
                + [_const_spec(a.shape) for a in mid]
                + [_select_spec(w_down_b.shape[1:], (o_idx,))])
    scratch = [
        pltpu.VMEM((ln + CONV_PAD, w), F32),
        pltpu.VMEM((ln, w), F32),
        pltpu.VMEM((ln, w), F32),
        pltpu.VMEM((ln, w), F32),
        pltpu.VMEM((MLSTM_HEADS, MLSTM_HEADDIM, MLSTM_HEADDIM), F32),
        pltpu.VMEM((MLSTM_HEADS, MLSTM_HEADDIM, MLSTM_HEADDIM), BF16),
        pltpu.VMEM((MLSTM_HEADS, 1, MLSTM_HEADDIM), F32),
        pltpu.VMEM((MLSTM_HEADS, 1, LANE), F32),
    ]
    return pl.pallas_call(
        _mlstm_kernel,
        grid=(bsz, seq // ln),
        in_specs=in_specs,
        out_specs=_tile_spec(ln, d),
        out_shape=jax.ShapeDtypeStruct((bsz, seq, d), F32),
        scratch_shapes=scratch,
        compiler_params=_params(("arbitrary", "arbitrary")),
        name="mlstm_mixer",
    )(x, mod3, row(g), w_up_b, *mid, w_down_b)


def kernel(x, c, ada_w, ada_b, norm_g, ffn_w_gate, ffn_w_up, ffn_w_down, hyb_w_in, hyb_w_out, lru_conv_w, lru_conv_b, lru_wa, lru_ba, lru_wx, lru_bx, lru_lambda, ssd_conv_w, ssd_conv_b, ssd_dt_bias, ssd_a_log, ssd_d, ssd_norm_g, mlstm_w_up, mlstm_conv_w, mlstm_conv_b, mlstm_wq, mlstm_wk, mlstm_wv, mlstm_w_gates, mlstm_b_gates, mlstm_norm_g, mlstm_skip, mlstm_w_down, final_norm_g):
    depth = ada_w.shape[0]
    mod = _ada_call(c, ada_w, ada_b)
    wg_b, wu_b, wd_b = _cast_bf16(ffn_w_gate), _cast_bf16(ffn_w_up), _cast_bf16(ffn_w_down)
    hyb_in_b, hyb_out_b = _cast_bf16(hyb_w_in), _cast_bf16(hyb_w_out)
    up_b, down_b = _cast_bf16(mlstm_w_up), _cast_bf16(mlstm_w_down)
    for layer in range(depth):
        x = _ffn_call(x, mod[layer, :, 0], norm_g[layer, 0], wg_b, wu_b, wd_b, final_norm_g,
                      lead=(layer, 0), final_norm=False)
        if layer % 2 == 0:
            e = layer // 2
            x = _hyb_call(x, mod[layer, :, 1], norm_g[layer, 1], hyb_in_b, hyb_out_b, e,
                          hyb_w_in[e][:, HYB_SPLITS[-1]:], lru_conv_w[e], lru_conv_b[e], lru_wa[e], lru_ba[e],
                          lru_wx[e], lru_bx[e], lru_lambda[e], ssd_conv_w[e], ssd_conv_b[e], ssd_dt_bias[e],
                          ssd_a_log[e], ssd_d[e], ssd_norm_g[e])
        else:
            o = layer // 2
            x = _mlstm_call(x, mod[layer, :, 1], norm_g[layer, 1], up_b, down_b, o, mlstm_conv_w[o], mlstm_conv_b[o],
                            mlstm_wq[o], mlstm_wk[o], mlstm_wv[o], mlstm_w_gates[o], mlstm_b_gates[o],
                            mlstm_norm_g[o], mlstm_skip[o])
        x = _ffn_call(x, mod[layer, :, 2], norm_g[layer, 2], wg_b, wu_b, wd_b, final_norm_g,
                      lead=(layer, 1), final_norm=(layer == depth - 1))
    return x
```

```python
import functools
import math

import jax
import jax.numpy as jnp
from jax import lax
from jax.experimental import pallas as pl
from jax.experimental.pallas import tpu as pltpu

F32 = jnp.float32
BF16 = jnp.bfloat16

D_MODEL = 1024
N_SUB = 3
HALF_STEP = 0.5
D_FF = 2816
CONV_WIDTH = 4
EPS = 1e-6
LRU_WIDTH = D_MODEL
LRU_HEADS = 8
LRU_BLOCK = LRU_WIDTH // LRU_HEADS
LRU_C = 8.0
SSD_WIDTH = D_MODEL
SSD_HEADDIM = 64
SSD_HEADS = SSD_WIDTH // SSD_HEADDIM
SSD_GROUPS = 2
SSD_STATE = 128
SSD_CHUNK = 128
SSD_CONV_DIM = SSD_WIDTH + 2 * SSD_GROUPS * SSD_STATE
HYB_SPLITS = (LRU_WIDTH, 2 * LRU_WIDTH, 2 * LRU_WIDTH + SSD_WIDTH, 2 * LRU_WIDTH + SSD_WIDTH + SSD_CONV_DIM)
MLSTM_WIDTH = 2 * D_MODEL
MLSTM_HEADS = 4
MLSTM_HEADDIM = MLSTM_WIDTH // MLSTM_HEADS
MLSTM_QKV_BLOCK = 4

LANE = 128
SUBLANE = 8
PACKED_ROWS = 16
MXU_DIM = 256
VMEM_LIMIT_BYTES = 56 * 1024 * 1024

FFN_TILE = 1024
FFN_PART = 256
HYB_TILE = 512
MLSTM_CHUNK = 256
ADA_TILE_N = 1536
CAST_ROWS = 512
CONV_PAD = SUBLANE
ROW_STRIP = PACKED_ROWS
CONV_STRIP = 64
LRU_COLS = 2 * LRU_BLOCK
SSD_COLS = 512


def _silu(x):
    return x * jax.nn.sigmoid(x)


def _softplus(x):
    return jnp.maximum(x, 0.0) + jnp.log1p(jnp.exp(-jnp.abs(x)))


def _gelu_tanh(x):
    c = math.sqrt(2.0 / math.pi)
    return x * (0.5 * (1.0 + jnp.tanh(c * (x + 0.044715 * (x * x * x)))))


def _modulate(x, g, shift, scale):
    y = x * lax.rsqrt(jnp.mean(x * x, axis=-1, keepdims=True) + EPS)
    return (y * g) * (1.0 + scale) + shift


def _modulate_rows(x_ref, h_ref, g, shift, scale, t):
    for r in range(t // ROW_STRIP):
        rows = slice(ROW_STRIP * r, ROW_STRIP * (r + 1))
        h_ref[rows, :] = _modulate(x_ref[0, rows, :], g, shift, scale).astype(BF16)


def _split_bf16(x, n):
    parts = []
    r = x
    for _ in range(n):
        p = r.astype(BF16)
        parts.append(p)
        r = r - p.astype(F32)
    return parts


def _dot(a, b):
    return jnp.dot(a, b, preferred_element_type=F32)


def _dot_lhs01(m01, x, n):
    out = None
    for p in _split_bf16(x, n):
        t = _dot(m01, p)
        out = t if out is None else out + t
    return out


def _dot_rhs01(x, m01, n):
    out = None
    for p in _split_bf16(x, n):
        t = _dot(p, m01)
        out = t if out is None else out + t
    return out


def _conv_rows(ext_ref, w_ref, b_ref, r0, nrows, cols):
    ntile = nrows // SUBLANE
    tiles = [ext_ref[r0 + SUBLANE * i:r0 + SUBLANE * (i + 1), cols] for i in range(ntile + 1)]
    row = lax.broadcasted_iota(jnp.int32, tiles[0].shape, 0)
    bias = b_ref[:, cols]
    w_now = w_ref[CONV_WIDTH - 1:CONV_WIDTH, cols]
    acc = [bias + w_now * tiles[i + 1] for i in range(ntile)]
    for s in range(1, CONV_WIDTH):
        w_s = w_ref[CONV_WIDTH - 1 - s:CONV_WIDTH - s, cols]
        rot = [pltpu.roll(tl, s, 0) for tl in tiles]
        head = row < s
        acc = [acc[i] + w_s * jnp.where(head, rot[i], rot[i + 1]) for i in range(ntile)]
    return jnp.concatenate(acc, axis=0)


def _const_spec(shape):
    nd = len(shape)
    return pl.BlockSpec(shape, lambda *_: (0,) * nd, pipeline_mode=pl.Buffered(1))


def _select_spec(block, lead):
    nl = len(lead)
    return pl.BlockSpec((None,) * nl + tuple(block), lambda *_: tuple(lead) + (0,) * len(block),
                        pipeline_mode=pl.Buffered(1))


def _tile_spec(t, d):
    return pl.BlockSpec((1, t, d), lambda b, i: (b, i, 0))


def _mod_spec(d):
    return pl.BlockSpec((1, 3, d), lambda b, i: (b, 0, 0))


def _params(semantics):
    return pltpu.CompilerParams(dimension_semantics=semantics, vmem_limit_bytes=VMEM_LIMIT_BYTES)


def _cast_kernel(w_ref, o_ref):
    o_ref[...] = w_ref[...].astype(BF16)


def _cast_bf16(w):
    shape = w.shape
    cols = shape[-1]
    rows = math.prod(shape[:-1])
    out = pl.pallas_call(
        _cast_kernel,
        grid=(rows // CAST_ROWS,),
        in_specs=[pl.BlockSpec((CAST_ROWS, cols), lambda i: (i, 0))],
        out_specs=pl.BlockSpec((CAST_ROWS, cols), lambda i: (i, 0)),
        out_shape=jax.ShapeDtypeStruct((rows, cols), BF16),
        compiler_params=_params(("arbitrary",)),
        name="cast_bf16",
    )(w.reshape(rows, cols))
    return out.reshape(shape)


def _ada_kernel(c_ref, w_ref, b_ref, o_ref):
    c = c_ref[...]
    ca = _silu(c)
    w = w_ref[0]
    c1, c2 = _split_bf16(ca, 2)
    w1, w2 = _split_bf16(w, 2)
    o_ref[0] = _dot(c1, w1) + (_dot(c1, w2) + _dot(c2, w1)) + b_ref[0]


def _ada_call(c, ada_w, ada_b):
    depth, d, n = ada_w.shape
    bsz = c.shape[0]
    rows = -(-bsz // SUBLANE) * SUBLANE
    c_pad = jnp.zeros((rows, d), F32).at[:bsz].set(c)
    out = pl.pallas_call(
        _ada_kernel,
        grid=(depth, n // ADA_TILE_N),
        in_specs=[pl.BlockSpec((rows, d), lambda l, j: (0, 0)),
                  pl.BlockSpec((1, d, ADA_TILE_N), lambda l, j: (l, 0, j)),
                  pl.BlockSpec((1, 1, ADA_TILE_N), lambda l, j: (l, 0, j))],
        out_specs=pl.BlockSpec((1, rows, ADA_TILE_N), lambda l, j: (l, 0, j)),
        out_shape=jax.ShapeDtypeStruct((depth, rows, n), F32),
        compiler_params=_params(("arbitrary", "arbitrary")),
        name="ada_mod",
    )(c_pad, ada_w, ada_b.reshape(depth, 1, n))
    return out[:, :bsz].reshape(depth, bsz, N_SUB, 3, d)


def _ffn_kernel(x_ref, mod_ref, g_ref, wg_ref, wu_ref, wd_ref, fg_ref, o_ref, *, final_norm):
    m = mod_ref[0]
    t = x_ref.shape[1]
    n = t // FFN_PART
    parts = [slice(FFN_PART * p, FFN_PART * (p + 1)) for p in range(n)]

    def norm(p):
        return _modulate(x_ref[0, parts[p], :], g_ref[...], m[0:1], m[1:2]).astype(BF16)

    def project(h):
        return _dot(h, wg_ref[...]), _dot(h, wu_ref[...])

    h_next = norm(0)
    gu_next = project(h_next)
    h_next = norm(1) if n > 1 else None
    for p in range(n):
        gate, up = gu_next
        if p + 1 < n:
            gu_next = project(h_next)
            h_next = norm(p + 2) if p + 2 < n else None
        act = (_silu(gate) * up).astype(BF16)
        y = _dot(act, wd_ref[...])
        o = x_ref[0, parts[p], :] + (HALF_STEP * (1.0 + m[2:3])) * y
        if final_norm:
            o = (o * lax.rsqrt(jnp.mean(o * o, axis=-1, keepdims=True) + EPS)) * fg_ref[...]
        o_ref[0, parts[p], :] = o


def _ffn_call(x, mod3, g, wg, wu, wd, final_g, *, lead, final_norm):
    bsz, seq, d = x.shape
    f = wg.shape[-1]
    t = FFN_TILE
    return pl.pallas_call(
        functools.partial(_ffn_kernel, final_norm=final_norm),
        grid=(bsz, seq // t),
        in_specs=[_tile_spec(t, d), _mod_spec(d), _const_spec((1, d)),
                  _select_spec((d, f), lead), _select_spec((d, f), lead), _select_spec((f, d), lead),
                  _const_spec((1, d))],
        out_specs=_tile_spec(t, d),
        out_shape=jax.ShapeDtypeStruct((bsz, seq, d), F32),
        compiler_params=_params(("arbitrary", "arbitrary")),
        name="ffn",
    )(x, mod3, g.reshape(1, d), wg, wu, wd, final_g.reshape(1, d))


def _lru_group_scan(a, u):
    row = lax.broadcasted_iota(jnp.int32, a.shape, 0)
    for d in (1, 2, 4):
        a_prev = pltpu.roll(a, d, 0)
        u_prev = pltpu.roll(u, d, 0)
        keep = row >= d
        u = jnp.where(keep, a * u_prev + u, u)
        a = jnp.where(keep, a * a_prev, a)
    return a, u


def _ssd_chunk(c, causal, tril, low_half, xbc_s, dt_s, z_s, sstate, ycat, alog_ref, dexp_ref, sng_ref, e_ref):
    ln = SSD_CHUNK
    gn = SSD_GROUPS * SSD_STATE
    gw = SSD_WIDTH // SSD_GROUPS
    hpg = SSD_HEADS // SSD_GROUPS
    rows = slice(c * ln, (c + 1) * ln)
    xs = xbc_s[rows, 0:SSD_WIDTH]
    bm = xbc_s[rows, SSD_WIDTH:SSD_WIDTH + gn]
    cm = xbc_s[rows, SSD_WIDTH + gn:SSD_WIDTH + 2 * gn]
    dt = dt_s[rows, :]
    ac = dt * (-jnp.exp(alog_ref[...]))
    acs = _dot_lhs01(tril, ac, 3)
    acs_last = acs[ln - 1:ln, :]
    acs_t = acs.T
    dt_t = dt.T
    e = e_ref[...]
    dec_out = _dot_rhs01(jnp.exp(acs), e, 1)
    w_st = _dot_rhs01(dt * jnp.exp(acs_last - acs), e, 1)
    d_a = _dot_rhs01(jnp.broadcast_to(jnp.exp(acs_last), (SUBLANE, LANE)), e, 2)[0:1]
    xw_b = (xs * w_st).astype(BF16)
    s_prev = sstate[...]
    s_prev_b = s_prev.astype(BF16)
    ys = []
    sts = []
    for g in range(SSD_GROUPS):
        bg = bm[:, SSD_STATE * g:SSD_STATE * (g + 1)]
        cg_b = cm[:, SSD_STATE * g:SSD_STATE * (g + 1)].astype(BF16)
        cb = lax.dot_general(cg_b, bg.astype(BF16), (((1,), (1,)), ((), ())),
                             preferred_element_type=F32)
        y_off = _dot(cg_b, s_prev_b[:, gw * g:gw * (g + 1)])
        sts.append(_dot(bg.T.astype(BF16), xw_b[:, gw * g:gw * (g + 1)]))
        for j in range(hpg // 2):
            pair = (hpg // 2) * g + j
            mats = []
            for hd in (2 * pair, 2 * pair + 1):
                diff = jnp.where(causal, acs[:, hd:hd + 1] - acs_t[hd:hd + 1, :], -jnp.inf)
                mats.append(cb * jnp.exp(diff) * dt_t[hd:hd + 1, :])
            lhs = jnp.concatenate(mats, axis=1).astype(BF16)
            xp = xs[:, LANE * pair:LANE * (pair + 1)]
            rhs = jnp.concatenate([jnp.where(low_half, xp, 0.0), jnp.where(low_half, 0.0, xp)],
                                  axis=0).astype(BF16)
            ys.append(_dot(lhs, rhs)
                      + y_off[:, LANE * j:LANE * (j + 1)] * dec_out[:, LANE * pair:LANE * (pair + 1)])
    sstate[...] = s_prev * d_a + jnp.concatenate(sts, axis=1)
    y = jnp.concatenate(ys, axis=1) + dexp_ref[...] * xs
    y = y * _silu(z_s[rows, :])
    for g in range(SSD_GROUPS):
        yg = y[:, gw * g:gw * (g + 1)]
        yn = (yg * lax.rsqrt(jnp.mean(yg * yg, axis=-1, keepdims=True) + EPS)) * sng_ref[:, gw * g:gw * (g + 1)]
        ycat[rows, LRU_WIDTH + gw * g:LRU_WIDTH + gw * (g + 1)] = yn.astype(BF16)


def _hyb_kernel(x_ref, mod_ref, g_ref, win_ref, wdt_ref, lcw_ref, lcb_ref, wai_ref, ba_ref, bx_ref, lam_ref,
                scw_ref, scb_ref, dtb_ref, alog_ref, dexp_ref, sng_ref, e_ref, wout_ref,
                o_ref,
                h_s, ext_l, xl_s, ri_s, gl_s, hcar, ext_s, xbc_s, dt_s, z_s, sstate, ycat):
    t = x_ref.shape[1]
    ln = SSD_CHUNK
    s0, s1, s2, s3 = HYB_SPLITS

    @pl.when(pl.program_id(1) == 0)
    def _():
        ext_l[0:CONV_PAD, :] = jnp.zeros((CONV_PAD, LRU_WIDTH), F32)
        ext_s[0:CONV_PAD, :] = jnp.zeros((CONV_PAD, SSD_CONV_DIM), F32)
        hcar[...] = jnp.zeros_like(hcar)
        sstate[...] = jnp.zeros_like(sstate)

    m = mod_ref[0]
    _modulate_rows(x_ref, h_s, g_ref[...], m[0:1], m[1:2], t)

    n_lru = LRU_WIDTH // LRU_COLS
    n_ssd = SSD_CONV_DIM // SSD_COLS

    def project_lru(p):
        cols = slice(LRU_COLS * p, LRU_COLS * (p + 1))
        ext_l[CONV_PAD:CONV_PAD + t, cols] = _dot(h_s[...], win_ref[:, s0 + LRU_COLS * p:s0 + LRU_COLS * (p + 1)])
        gl_s[:, cols] = _dot(h_s[...], win_ref[:, LRU_COLS * p:LRU_COLS * (p + 1)])

    def project_ssd(q):
        cols = slice(SSD_COLS * q, SSD_COLS * (q + 1))
        ext_s[CONV_PAD:CONV_PAD + t, cols] = _dot(h_s[...], win_ref[:, s2 + SSD_COLS * q:s2 + SSD_COLS * (q + 1)])

    project_lru(0)
    for p in range(n_lru):
        cols = slice(LRU_COLS * p, LRU_COLS * (p + 1))
        rcols = slice(2 * LRU_COLS * p, 2 * LRU_COLS * p + LRU_COLS)
        icols = slice(2 * LRU_COLS * p + LRU_COLS, 2 * LRU_COLS * (p + 1))
        if p + 1 < n_lru:
            project_lru(p + 1)
        else:
            project_ssd(0)
        for r in range(t // CONV_STRIP):
            xl_s[CONV_STRIP * r:CONV_STRIP * (r + 1), cols] = _conv_rows(ext_l, lcw_ref, lcb_ref, CONV_STRIP * r,
                                                                          CONV_STRIP, cols)
        ext_l[0:CONV_PAD, cols] = ext_l[t:t + CONV_PAD, cols]
        ri_s[:, 2 * LRU_COLS * p:2 * LRU_COLS * (p + 1)] = _dot(xl_s[:, cols].astype(BF16), wai_ref[p])
        sp = LRU_C * _softplus(-lam_ref[:, cols])
        ba = ba_ref[:, cols]
        bx = bx_ref[:, cols]
        carry = hcar[:, cols]
        for r in range(t // ROW_STRIP):
            rows = slice(ROW_STRIP * r, ROW_STRIP * (r + 1))
            rg = jax.nn.sigmoid(ri_s[rows, rcols] + ba)
            ig = jax.nn.sigmoid(ri_s[rows, icols] + bx)
            nlog_a = rg * sp
            a = jnp.exp(-nlog_a)
            u = jnp.sqrt(jnp.tanh(nlog_a) * (a * a + 1.0)) * (ig * xl_s[rows, cols])
            hs = []
            for q in range(ROW_STRIP // SUBLANE):
                a8, u8 = _lru_group_scan(a[SUBLANE * q:SUBLANE * (q + 1)], u[SUBLANE * q:SUBLANE * (q + 1)])
                h8 = a8 * carry + u8
                carry = h8[SUBLANE - 1:SUBLANE, :]
                hs.append(h8)
            ycat[rows, cols] = (jnp.concatenate(hs, axis=0) * _gelu_tanh(gl_s[rows, cols])).astype(BF16)
        hcar[:, cols] = carry

    for q in range(n_ssd):
        cols = slice(SSD_COLS * q, SSD_COLS * (q + 1))
        if q + 1 < n_ssd:
            project_ssd(q + 1)
        else:
            z_s[...] = _dot(h_s[...], win_ref[:, s1:s2])
            dt_s[...] = _softplus(_dot(h_s[...], wdt_ref[...]) + dtb_ref[...])
        for r in range(t // CONV_STRIP):
            xbc_s[CONV_STRIP * r:CONV_STRIP * (r + 1), cols] = _silu(
                _conv_rows(ext_s, scw_ref, scb_ref, CONV_STRIP * r, CONV_STRIP, cols))
        ext_s[0:CONV_PAD, cols] = ext_s[t:t + CONV_PAD, cols]
    y_lru = _dot(ycat[:, 0:LRU_WIDTH], wout_ref[0:LRU_WIDTH, :])

    ri_ = lax.broadcasted_iota(jnp.int32, (ln, ln), 0)
    ci_ = lax.broadcasted_iota(jnp.int32, (ln, ln), 1)
    causal = ci_ <= ri_
    tril = jnp.where(causal, 1.0, 0.0).astype(BF16)
    low_half = lax.broadcasted_iota(jnp.int32, (ln, LANE), 1) < SSD_HEADDIM
    for c in range(t // ln):
        _ssd_chunk(c, causal, tril, low_half, xbc_s, dt_s, z_s, sstate, ycat, alog_ref, dexp_ref, sng_ref, e_ref)

    y_mix = y_lru + _dot(ycat[:, LRU_WIDTH:], wout_ref[LRU_WIDTH:, :])
    o_ref[0] = x_ref[0] + (1.0 + m[2:3]) * y_mix


def _hyb_call(x, mod3, g, w_in_b, w_out_b, e_idx, w_dt, lru_conv_w, lru_conv_b, lru_wa, lru_ba, lru_wx, lru_bx,
              lru_lambda, ssd_conv_w, ssd_conv_b, ssd_dt_bias, ssd_a_log, ssd_d, ssd_norm_g):
    bsz, seq, d = x.shape
    t = HYB_TILE
    s3 = HYB_SPLITS[-1]
    w_dt_p = jnp.zeros((d, LANE), BF16).at[:, :SSD_HEADS].set(w_dt.astype(BF16))
    npair = LRU_HEADS // 2
    za = jnp.zeros((npair, LRU_BLOCK, LRU_BLOCK), F32)
    wa2 = lru_wa.reshape(npair, 2, LRU_BLOCK, LRU_BLOCK)
    wx2 = lru_wx.reshape(npair, 2, LRU_BLOCK, LRU_BLOCK)
    wai = jnp.concatenate([jnp.concatenate([wa2[:, 0], za, wx2[:, 0], za], axis=-1),
                           jnp.concatenate([za, wa2[:, 1], za, wx2[:, 1]], axis=-1)], axis=1).astype(BF16)
    pad_h = lambda v: jnp.zeros((1, LANE), F32).at[0, :SSD_HEADS].set(v)
    expand = jnp.zeros((LANE, SSD_WIDTH), F32).at[:SSD_HEADS].set(
        jnp.repeat(jnp.eye(SSD_HEADS, dtype=F32), SSD_HEADDIM, axis=1)).astype(BF16)
    row = lambda v: v.reshape(1, -1)
    small = (row(g),)
    rest = (w_dt_p, lru_conv_w, row(lru_conv_b), wai, row(lru_ba), row(lru_bx), row(lru_lambda),
            ssd_conv_w, row(ssd_conv_b), pad_h(ssd_dt_bias), pad_h(ssd_a_log),
            row(jnp.repeat(ssd_d, SSD_HEADDIM)), row(ssd_norm_g), expand)
    in_specs = ([_tile_spec(t, d), _mod_spec(d), _const_spec((1, d)), _select_spec((d, s3), (e_idx,))]
                + [_const_spec(a.shape) for a in rest]
                + [_select_spec(w_out_b.shape[1:], (e_idx,))])
    scratch = [
        pltpu.VMEM((t, d), BF16),
        pltpu.VMEM((t + CONV_PAD, LRU_WIDTH), F32),
        pltpu.VMEM((t, LRU_WIDTH), F32),
        pltpu.VMEM((t, 2 * LRU_WIDTH), F32),
        pltpu.VMEM((t, LRU_WIDTH), F32),
        pltpu.VMEM((1, LRU_WIDTH), F32),
        pltpu.VMEM((t + CONV_PAD, SSD_CONV_DIM), F32),
        pltpu.VMEM((t, SSD_CONV_DIM), F32),
        pltpu.VMEM((t, LANE), F32),
        pltpu.VMEM((t, SSD_WIDTH), F32),
        pltpu.VMEM((SSD_STATE, SSD_WIDTH), F32),
        pltpu.VMEM((t, LRU_WIDTH + SSD_WIDTH), BF16),
    ]
    return pl.pallas_call(
        _hyb_kernel,
        grid=(bsz, seq // t),
        in_specs=in_specs,
        out_specs=_tile_spec(t, d),
        out_shape=jax.ShapeDtypeStruct((bsz, seq, d), F32),
        scratch_shapes=scratch,
        compiler_params=_params(("arbitrary", "arbitrary")),
        name="hybrid_mixer",
    )(x, mod3, *small, w_in_b, *rest, w_out_b)


def _mlstm_kernel(x_ref, mod_ref, g_ref, wup_ref, cw_ref, cb_ref, wqk_ref, wv_ref, wg_ref, bg_ref,
                  ng_ref, skip_ref, wdown_ref,
                  o_ref,
                  ext, q_s, k_s, v_s, cst, cst_b, nst, mst):
    ln = x_ref.shape[1]
    w = MLSTM_WIDTH
    dh = MLSTM_HEADDIM
    nblk = w // MXU_DIM

    @pl.when(pl.program_id(1) == 0)
    def _():
        ext[0:CONV_PAD, :] = jnp.zeros((CONV_PAD, w), F32)
        cst[...] = jnp.zeros_like(cst)
        cst_b[...] = jnp.zeros_like(cst_b)
        nst[...] = jnp.zeros_like(nst)
        mst[...] = jnp.zeros_like(mst)

    x = x_ref[0]
    m = mod_ref[0]
    h = _modulate(x, g_ref[...], m[0:1], m[1:2]).astype(BF16)
    xm = _dot(h, wup_ref[:, 0:w])
    z = _dot(h, wup_ref[:, w:2 * w])
    ext[CONV_PAD:CONV_PAD + ln, :] = xm
    xc = _silu(_conv_rows(ext, cw_ref, cb_ref, 0, ln, slice(0, w)))
    ext[0:CONV_PAD, :] = xm[ln - CONV_PAD:ln, :]
    xc_b = xc.astype(BF16)
    xm_b = xm.astype(BF16)
    for b in range(nblk):
        sl = slice(MXU_DIM * b, MXU_DIM * (b + 1))
        qk = _dot(xc_b[:, sl], wqk_ref[b])
        q_s[:, sl] = qk[:, :MXU_DIM]
        k_s[:, sl] = qk[:, MXU_DIM:]
        v_s[:, sl] = _dot(xm_b[:, sl], wv_ref[b])
    gates = (_dot(q_s[...].astype(BF16), wg_ref[0:w, :]) + _dot(k_s[...].astype(BF16), wg_ref[w:2 * w, :])
             + _dot(v_s[...].astype(BF16), wg_ref[2 * w:3 * w, :]) + bg_ref[...])
    log_f = -_softplus(-gates)
    ri_ = lax.broadcasted_iota(jnp.int32, (ln, ln), 0)
    ci_ = lax.broadcasted_iota(jnp.int32, (ln, ln), 1)
    causal = ci_ <= ri_
    tril = jnp.where(causal, 1.0, 0.0).astype(BF16)
    bcum = _dot_lhs01(tril, log_f, 3)
    gates_t = gates.T
    bcum_t = bcum.T
    k_scale = dh ** -0.5
    heads = range(MLSTM_HEADS)
    hsl = [slice(dh * hd, dh * (hd + 1)) for hd in heads]
    st = []
    for hd in heads:
        fi = MLSTM_HEADS + hd
        bc_col = bcum[:, fi:fi + 1]
        bc_row = bcum_t[fi:fi + 1, :]
        i_col = gates[:, hd:hd + 1]
        i_row = gates_t[hd:hd + 1, :]
        m_prev = mst[hd][:, 0:1]
        log_d = jnp.where(causal, bc_col - bc_row + i_row, -jnp.inf)
        g_col = bc_col + m_prev
        m_rows = jnp.maximum(g_col, jnp.max(log_d, axis=-1, keepdims=True))
        b_last = bc_col[ln - 1:ln, :]
        log_w = b_last - bc_col + i_col
        m_next = jnp.maximum(b_last + m_prev, jnp.max(log_w, axis=0, keepdims=True))
        st.append(dict(m_rows=m_rows, w_inter=jnp.exp(g_col - m_rows), d_mat=jnp.exp(log_d - m_rows),
                       w_s=jnp.exp(log_w - m_next), decay=jnp.exp(b_last + m_prev - m_next), m_next=m_next))
    for hd in heads:
        s = st[hd]
        q_h = q_s[:, hsl[hd]]
        k_h = k_s[:, hsl[hd]] * k_scale
        q_b = q_h.astype(BF16)
        v_b = v_s[:, hsl[hd]].astype(BF16)
        kw = k_h * s["w_s"]
        s.update(q_h=q_h, v_b=v_b, kw_sum=jnp.sum(kw, axis=0, keepdims=True),
                 qk=lax.dot_general(q_b, k_h.astype(BF16), (((1,), (1,)), ((), ())), preferred_element_type=F32),
                 qc=_dot(q_b, cst_b[hd]),
                 upd=lax.dot_general(kw.astype(BF16), v_b, (((0,), (0,)), ((), ())), preferred_element_type=F32))
    for hd in heads:
        s = st[hd]
        s_mat = s["qk"] * s["d_mat"]
        n_prev = nst[hd]
        num = s["w_inter"] * s["qc"] + _dot(s_mat.astype(BF16), s["v_b"])
        den = (s["w_inter"] * jnp.sum(s["q_h"] * n_prev, axis=-1, keepdims=True)
               + jnp.sum(s_mat, axis=-1, keepdims=True))
        s["h_out"] = num / jnp.maximum(jnp.abs(den), jnp.exp(-s["m_rows"]))
        c_new = s["decay"] * cst[hd] + s["upd"]
        cst[hd] = c_new
        cst_b[hd] = c_new.astype(BF16)
        nst[hd] = s["decay"] * n_prev + s["kw_sum"]
        mst[hd] = jnp.broadcast_to(s["m_next"], (1, LANE))
    y_mix = None
    for hd in heads:
        sl = hsl[hd]
        h_out = st[hd]["h_out"]
        mu = jnp.mean(h_out, axis=-1, keepdims=True)
        hc = h_out - mu
        var = jnp.mean(hc * hc, axis=-1, keepdims=True)
        hn = (hc * lax.rsqrt(var + EPS)) * ng_ref[:, sl]
        out = (hn + skip_ref[:, sl] * xc[:, sl]) * _silu(z[:, sl])
        y_head = _dot(out.astype(BF16), wdown_ref[sl, :])
        y_mix = y_head if y_mix is None else y_mix + y_head

    o_ref[0] = x + (1.0 + m[2:3]) * y_mix


def _expand_block_diag(w, group):
    nb, bi, bo = w.shape
    rows = w.reshape(nb // group, group * bi, bo)
    r_idx = jnp.arange(group * bi)[:, None]
    c_idx = jnp.arange(group * bo)[None, :]
    period = (c_idx % bo == jnp.arange(bo)[:, None]).astype(w.dtype)
    tiled = jnp.einsum('gro,oc->grc', rows, period, precision=lax.Precision.HIGHEST)
    return jnp.where((r_idx // bi == c_idx // bo)[None], tiled, 0.0)


def _mlstm_call(x, mod3, g, w_up_b, w_down_b, o_idx, conv_w, conv_b, wq, wk, wv, w_gates, b_gates, norm_g, skip):
    bsz, seq, d = x.shape
    ln = MLSTM_CHUNK
    w = MLSTM_WIDTH
    grp = MXU_DIM // MLSTM_QKV_BLOCK
    wqk = jnp.concatenate([_expand_block_diag(wq, grp), _expand_block_diag(wk, grp)], axis=-1).astype(BF16)
    wvd = _expand_block_diag(wv, grp).astype(BF16)
    wg_p = jnp.zeros((3 * w, LANE), BF16).at[:, :2 * MLSTM_HEADS].set(w_gates.astype(BF16))
    bg_p = jnp.zeros((1, LANE), F32).at[0, :2 * MLSTM_HEADS].set(b_gates)
    row = lambda v: v.reshape(1, -1)
    mid = (conv_w, row(conv_b), wqk, wvd, wg_p, bg_p, row(norm_g), row(skip))
    in_specs = ([_tile_spec(ln, d), _mod_spec(d), _const_spec((1, d)), _select_spec(w_up_b.shape[1:], (o_idx,))]
                + [_const_spec(a.shape) for a in mid]
                + [_select_spec(w_down_b.shape[1:], (o_idx,))])
    scratch = [
        pltpu.VMEM((ln + CONV_PAD, w), F32),
        pltpu.VMEM((ln, w), F32),
        pltpu.VMEM((ln, w), F32),
        pltpu.VMEM((ln, w), F32),
        pltpu.VMEM((MLSTM_HEADS, MLSTM_HEADDIM, MLSTM_HEADDIM), F32),
        pltpu.VMEM((MLSTM_HEADS, MLSTM_HEADDIM, MLSTM_HEADDIM), BF16),
        pltpu.VMEM((MLSTM_HEADS, 1, MLSTM_HEADDIM), F32),
        pltpu.VMEM((MLSTM_HEADS, 1, LANE), F32),
    ]
    return pl.pallas_call(
        _mlstm_kernel,
        grid=(bsz, seq // ln),
        in_specs=in_specs,
        out_specs=_tile_spec(ln, d),
        out_shape=jax.ShapeDtypeStruct((bsz, seq, d), F32),
        scratch_shapes=scratch,
        compiler_params=_params(("arbitrary", "arbitrary")),
        name="mlstm_mixer",
    )(x, mod3, row(g), w_up_b, *mid, w_down_b)


def kernel(x, c, ada_w, ada_b, norm_g, ffn_w_gate, ffn_w_up, ffn_w_down, hyb_w_in, hyb_w_out, lru_conv_w, lru_conv_b, lru_wa, lru_ba, lru_wx, lru_bx, lru_lambda, ssd_conv_w, ssd_conv_b, ssd_dt_bias, ssd_a_log, ssd_d, ssd_norm_g, mlstm_w_up, mlstm_conv_w, mlstm_conv_b, mlstm_wq, mlstm_wk, mlstm_wv, mlstm_w_gates, mlstm_b_gates, mlstm_norm_g, mlstm_skip, mlstm_w_down, final_norm_g):
    depth = ada_w.shape[0]
    mod = _ada_call(c, ada_w, ada_b)
    wg_b, wu_b, wd_b = _cast_bf16(ffn_w_gate), _cast_bf16(ffn_w_up), _cast_bf16(ffn_w_down)
    hyb_in_b, hyb_out_b = _cast_bf16(hyb_w_in), _cast_bf16(hyb_w_out)
    up_b, down_b = _cast_bf16(mlstm_w_up), _cast_bf16(mlstm_w_down)
    for layer in range(depth):
        x = _ffn_call(x, mod[layer, :, 0], norm_g[layer, 0], wg_b, wu_b, wd_b, final_norm_g,
                      lead=(layer, 0), final_norm=False)
        if layer % 2 == 0:
            e = layer // 2
            x = _hyb_call(x, mod[layer, :, 1], norm_g[layer, 1], hyb_in_b, hyb_out_b, e,
                          hyb_w_in[e][:, HYB_SPLITS[-1]:], lru_conv_w[e], lru_conv_b[e], lru_wa[e], lru_ba[e],
                          lru_wx[e], lru_bx[e], lru_lambda[e], ssd_conv_w[e], ssd_conv_b[e], ssd_dt_bias[e],
                          ssd_a_log[e], ssd_d[e], ssd_norm_g[e])
        else:
            o = layer // 2
            x = _mlstm_call(x, mod[layer, :, 1], norm_g[layer, 1], up_b, down_b, o, mlstm_conv_w[o], mlstm_conv_b[o],
                            mlstm_wq[o], mlstm_wk[o], mlstm_wv[o], mlstm_w_gates[o], mlstm_b_gates[o],
                            mlstm_norm_g[o], mlstm_skip[o])
        x = _ffn_call(x, mod[layer, :, 2], norm_g[layer, 2], wg_b, wu_b, wd_b, final_norm_g,
                      lead=(layer, 1), final_norm=(layer == depth - 1))
    return x
```

```python
import functools
import math

import jax
import jax.numpy as jnp
from jax import lax
from jax.experimental import pallas as pl
from jax.experimental.pallas import tpu as pltpu

F32 = jnp.float32
BF16 = jnp.bfloat16

D_MODEL = 1024
N_SUB = 3
HALF_STEP = 0.5
D_FF = 2816
CONV_WIDTH = 4
EPS = 1e-6
LRU_WIDTH = D_MODEL
LRU_HEADS = 8
LRU_BLOCK = LRU_WIDTH // LRU_HEADS
LRU_C = 8.0
SSD_WIDTH = D_MODEL
SSD_HEADDIM = 64
SSD_HEADS = SSD_WIDTH // SSD_HEADDIM
SSD_GROUPS = 2
SSD_STATE = 128
SSD_CHUNK = 128
SSD_CONV_DIM = SSD_WIDTH + 2 * SSD_GROUPS * SSD_STATE
HYB_SPLITS = (LRU_WIDTH, 2 * LRU_WIDTH, 2 * LRU_WIDTH + SSD_WIDTH, 2 * LRU_WIDTH + SSD_WIDTH + SSD_CONV_DIM)
MLSTM_WIDTH = 2 * D_MODEL
MLSTM_HEADS = 4
MLSTM_HEADDIM = MLSTM_WIDTH // MLSTM_HEADS
MLSTM_QKV_BLOCK = 4

LANE = 128
SUBLANE = 8
PACKED_ROWS = 16
MXU_DIM = 256
VMEM_LIMIT_BYTES = 56 * 1024 * 1024

FFN_TILE = 1024
FFN_PART = 256
HYB_TILE = 512
MLSTM_CHUNK = 256
ADA_TILE_N = 1536
CAST_ROWS = 512
CONV_PAD = SUBLANE
ROW_STRIP = PACKED_ROWS
CONV_STRIP = 64
LRU_COLS = 2 * LRU_BLOCK
SSD_COLS = 512


def _silu(x):
    return x * jax.nn.sigmoid(x)


def _softplus(x):
    return jnp.maximum(x, 0.0) + jnp.log1p(jnp.exp(-jnp.abs(x)))


def _gelu_tanh(x):
    c = math.sqrt(2.0 / math.pi)
    return x * (0.5 * (1.0 + jnp.tanh(c * (x + 0.044715 * (x * x * x)))))


def _modulate(x, g, shift, scale):
    y = x * lax.rsqrt(jnp.mean(x * x, axis=-1, keepdims=True) + EPS)
    return (y * g) * (1.0 + scale) + shift


def _modulate_rows(x_ref, h_ref, g, shift, scale, t):
    for r in range(t // ROW_STRIP):
        rows = slice(ROW_STRIP * r, ROW_STRIP * (r + 1))
        h_ref[rows, :] = _modulate(x_ref[0, rows, :], g, shift, scale).astype(BF16)


def _split_bf16(x, n):
    parts = []
    r = x
    for _ in range(n):
        p = r.astype(BF16)
        parts.append(p)
        r = r - p.astype(F32)
    return parts


def _dot(a, b):
    return jnp.dot(a, b, preferred_element_type=F32)


def _dot_lhs01(m01, x, n):
    out = None
    for p in _split_bf16(x, n):
        t = _dot(m01, p)
        out = t if out is None else out + t
    return out


def _dot_rhs01(x, m01, n):
    out = None
    for p in _split_bf16(x, n):
        t = _dot(p, m01)
        out = t if out is None else out + t
    return out


def _conv_rows(ext_ref, w_ref, b_ref, r0, nrows, cols):
    ntile = nrows // SUBLANE
    tiles = [ext_ref[r0 + SUBLANE * i:r0 + SUBLANE * (i + 1), cols] for i in range(ntile + 1)]
    row = lax.broadcasted_iota(jnp.int32, tiles[0].shape, 0)
    bias = b_ref[:, cols]
    w_now = w_ref[CONV_WIDTH - 1:CONV_WIDTH, cols]
    acc = [bias + w_now * tiles[i + 1] for i in range(ntile)]
    for s in range(1, CONV_WIDTH):
        w_s = w_ref[CONV_WIDTH - 1 - s:CONV_WIDTH - s, cols]
        rot = [pltpu.roll(tl, s, 0) for tl in tiles]
        head = row < s
        acc = [acc[i] + w_s * jnp.where(head, rot[i], rot[i + 1]) for i in range(ntile)]
    return jnp.concatenate(acc, axis=0)


def _pack_rows(entries, width):
    last = max(r + (a.shape[0] if a.ndim == 2 else 1) for r, a in entries)
    out = jnp.zeros((-(-last // SUBLANE) * SUBLANE, width), F32)
    for r, a in entries:
        a2 = a.reshape(1, -1) if a.ndim == 1 else a
        out = out.at[r:r + a2.shape[0], :a2.shape[1]].set(a2)
    return out


class _RowView:
    def __init__(self, ref, r0, n, width):
        self.ref, self.r0, self.n, self.width = ref, r0, n, width

    def __getitem__(self, idx):
        rows, cols = (slice(None), slice(None)) if idx is Ellipsis else idx
        r_lo, r_hi, _ = rows.indices(self.n)
        c_lo, c_hi, _ = cols.indices(self.width)
        return self.ref[self.r0 + r_lo:self.r0 + r_hi, c_lo:c_hi]


def _const_spec(shape):
    nd = len(shape)
    return pl.BlockSpec(shape, lambda *_: (0,) * nd, pipeline_mode=pl.Buffered(1))


def _select_spec(block, lead):
    nl = len(lead)
    return pl.BlockSpec((None,) * nl + tuple(block), lambda *_: tuple(lead) + (0,) * len(block),
                        pipeline_mode=pl.Buffered(1))


def _tile_spec(t, d):
    return pl.BlockSpec((1, t, d), lambda b, i: (b, i, 0))


def _mod_spec(d):
    return pl.BlockSpec((1, 3, d), lambda b, i: (b, 0, 0))


def _params(semantics):
    return pltpu.CompilerParams(dimension_semantics=semantics, vmem_limit_bytes=VMEM_LIMIT_BYTES)


def _cast_kernel(w_ref, o_ref):
    o_ref[...] = w_ref[...].astype(BF16)


def _cast_bf16(w):
    shape = w.shape
    cols = shape[-1]
    rows = math.prod(shape[:-1])
    out = pl.pallas_call(
        _cast_kernel,
        grid=(rows // CAST_ROWS,),
        in_specs=[pl.BlockSpec((CAST_ROWS, cols), lambda i: (i, 0))],
        out_specs=pl.BlockSpec((CAST_ROWS, cols), lambda i: (i, 0)),
        out_shape=jax.ShapeDtypeStruct((rows, cols), BF16),
        compiler_params=_params(("arbitrary",)),
        name="cast_bf16",
    )(w.reshape(rows, cols))
    return out.reshape(shape)


def _ada_kernel(c_ref, w_ref, b_ref, o_ref):
    c = c_ref[...]
    ca = _silu(c)
    w = w_ref[0]
    c1, c2 = _split_bf16(ca, 2)
    w1, w2 = _split_bf16(w, 2)
    o_ref[0] = _dot(c1, w1) + (_dot(c1, w2) + _dot(c2, w1)) + b_ref[0]


def _ada_call(c, ada_w, ada_b):
    depth, d, n = ada_w.shape
    bsz = c.shape[0]
    rows = -(-bsz // SUBLANE) * SUBLANE
    c_pad = jnp.zeros((rows, d), F32).at[:bsz].set(c)
    out = pl.pallas_call(
        _ada_kernel,
        grid=(depth, n // ADA_TILE_N),
        in_specs=[pl.BlockSpec((rows, d), lambda l, j: (0, 0)),
                  pl.BlockSpec((1, d, ADA_TILE_N), lambda l, j: (l, 0, j)),
                  pl.BlockSpec((1, 1, ADA_TILE_N), lambda l, j: (l, 0, j))],
        out_specs=pl.BlockSpec((1, rows, ADA_TILE_N), lambda l, j: (l, 0, j)),
        out_shape=jax.ShapeDtypeStruct((depth, rows, n), F32),
        compiler_params=_params(("arbitrary", "arbitrary")),
        name="ada_mod",
    )(c_pad, ada_w, ada_b.reshape(depth, 1, n))
    return out[:, :bsz].reshape(depth, bsz, N_SUB, 3, d)


def _ffn_kernel(x_ref, mod_ref, g_ref, wg_ref, wu_ref, wd_ref, fg_ref, o_ref, *, final_norm):
    m = mod_ref[0]
    t = x_ref.shape[1]
    n = t // FFN_PART
    parts = [slice(FFN_PART * p, FFN_PART * (p + 1)) for p in range(n)]

    def norm(p):
        return _modulate(x_ref[0, parts[p], :], g_ref[...], m[0:1], m[1:2]).astype(BF16)

    def project(h):
        return _dot(h, wg_ref[...]), _dot(h, wu_ref[...])

    h_next = norm(0)
    gu_next = project(h_next)
    h_next = norm(1) if n > 1 else None
    for p in range(n):
        gate, up = gu_next
        if p + 1 < n:
            gu_next = project(h_next)
            h_next = norm(p + 2) if p + 2 < n else None
        act = (_silu(gate) * up).astype(BF16)
        y = _dot(act, wd_ref[...])
        o = x_ref[0, parts[p], :] + (HALF_STEP * (1.0 + m[2:3])) * y
        if final_norm:
            o = (o * lax.rsqrt(jnp.mean(o * o, axis=-1, keepdims=True) + EPS)) * fg_ref[...]
        o_ref[0, parts[p], :] = o


def _ffn_call(x, mod3, g, wg, wu, wd, final_g, *, lead, final_norm):
    bsz, seq, d = x.shape
    f = wg.shape[-1]
    t = FFN_TILE
    return pl.pallas_call(
        functools.partial(_ffn_kernel, final_norm=final_norm),
        grid=(bsz, seq // t),
        in_specs=[_tile_spec(t, d), _mod_spec(d), _const_spec((1, d)),
                  _select_spec((d, f), lead), _select_spec((d, f), lead), _select_spec((f, d), lead),
                  _const_spec((1, d))],
        out_specs=_tile_spec(t, d),
        out_shape=jax.ShapeDtypeStruct((bsz, seq, d), F32),
        compiler_params=_params(("arbitrary", "arbitrary")),
        name="ffn",
    )(x, mod3, g.reshape(1, d), wg, wu, wd, final_g.reshape(1, d))


def _lru_group_scan(a, u):
    row = lax.broadcasted_iota(jnp.int32, a.shape, 0)
    for d in (1, 2, 4):
        a_prev = pltpu.roll(a, d, 0)
        u_prev = pltpu.roll(u, d, 0)
        keep = row >= d
        u = jnp.where(keep, a * u_prev + u, u)
        a = jnp.where(keep, a * a_prev, a)
    return a, u


def _ssd_chunk(c, causal, tril, low_half, xbc_s, dt_s, z_s, sstate, ycat, alog_ref, dexp_ref, sng_ref, e_ref):
    ln = SSD_CHUNK
    gn = SSD_GROUPS * SSD_STATE
    gw = SSD_WIDTH // SSD_GROUPS
    hpg = SSD_HEADS // SSD_GROUPS
    rows = slice(c * ln, (c + 1) * ln)
    xs = xbc_s[rows, 0:SSD_WIDTH]
    bm = xbc_s[rows, SSD_WIDTH:SSD_WIDTH + gn]
    cm = xbc_s[rows, SSD_WIDTH + gn:SSD_WIDTH + 2 * gn]
    dt = dt_s[rows, :]
    ac = dt * (-jnp.exp(alog_ref[...]))
    acs = _dot_lhs01(tril, ac, 3)
    acs_last = acs[ln - 1:ln, :]
    acs_t = acs.T
    dt_t = dt.T
    e = e_ref[...]
    dec_out = _dot_rhs01(jnp.exp(acs), e, 1)
    w_st = _dot_rhs01(dt * jnp.exp(acs_last - acs), e, 1)
    d_a = _dot_rhs01(jnp.broadcast_to(jnp.exp(acs_last), (SUBLANE, LANE)), e, 2)[0:1]
    xw_b = (xs * w_st).astype(BF16)
    s_prev = sstate[...]
    s_prev_b = s_prev.astype(BF16)
    ys = []
    sts = []
    for g in range(SSD_GROUPS):
        bg = bm[:, SSD_STATE * g:SSD_STATE * (g + 1)]
        cg_b = cm[:, SSD_STATE * g:SSD_STATE * (g + 1)].astype(BF16)
        cb = lax.dot_general(cg_b, bg.astype(BF16), (((1,), (1,)), ((), ())),
                             preferred_element_type=F32)
        y_off = _dot(cg_b, s_prev_b[:, gw * g:gw * (g + 1)])
        sts.append(_dot(bg.T.astype(BF16), xw_b[:, gw * g:gw * (g + 1)]))
        for j in range(hpg // 2):
            pair = (hpg // 2) * g + j
            mats = []
            for hd in (2 * pair, 2 * pair + 1):
                diff = jnp.where(causal, acs[:, hd:hd + 1] - acs_t[hd:hd + 1, :], -jnp.inf)
                mats.append(cb * jnp.exp(diff) * dt_t[hd:hd + 1, :])
            lhs = jnp.concatenate(mats, axis=1).astype(BF16)
            xp = xs[:, LANE * pair:LANE * (pair + 1)]
            rhs = jnp.concatenate([jnp.where(low_half, xp, 0.0), jnp.where(low_half, 0.0, xp)],
                                  axis=0).astype(BF16)
            ys.append(_dot(lhs, rhs)
                      + y_off[:, LANE * j:LANE * (j + 1)] * dec_out[:, LANE * pair:LANE * (pair + 1)])
    sstate[...] = s_prev * d_a + jnp.concatenate(sts, axis=1)
    y = jnp.concatenate(ys, axis=1) + dexp_ref[...] * xs
    y = y * _silu(z_s[rows, :])
    for g in range(SSD_GROUPS):
        yg = y[:, gw * g:gw * (g + 1)]
        yn = (yg * lax.rsqrt(jnp.mean(yg * yg, axis=-1, keepdims=True) + EPS)) * sng_ref[:, gw * g:gw * (g + 1)]
        ycat[rows, LRU_WIDTH + gw * g:LRU_WIDTH + gw * (g + 1)] = yn.astype(BF16)


HYB_ROWS = dict(g=0, lcb=1, ba=2, bx=3, lam=4, dexp=5, sng=6, scb=7, dtb=8, alog=9, lcw=10, scw=14)


def _hyb_kernel(x_ref, mod_ref, small_ref, win_ref, wdt_ref, wai_ref, e_ref, wout_ref,
                o_ref,
                h_s, ext_l, xl_s, ri_s, gl_s, hcar, ext_s, xbc_s, dt_s, z_s, sstate, ycat):
    t = x_ref.shape[1]
    ln = SSD_CHUNK
    s0, s1, s2, s3 = HYB_SPLITS

    def rows(name, n, width):
        return _RowView(small_ref, HYB_ROWS[name], n, width)

    g_ref, lcb_ref, ba_ref, bx_ref, lam_ref = (rows(k, 1, LRU_WIDTH) for k in ("g", "lcb", "ba", "bx", "lam"))
    dexp_ref, sng_ref = rows("dexp", 1, SSD_WIDTH), rows("sng", 1, SSD_WIDTH)
    scb_ref, dtb_ref, alog_ref = rows("scb", 1, SSD_CONV_DIM), rows("dtb", 1, LANE), rows("alog", 1, LANE)
    lcw_ref, scw_ref = rows("lcw", CONV_WIDTH, LRU_WIDTH), rows("scw", CONV_WIDTH, SSD_CONV_DIM)

    @pl.when(pl.program_id(1) == 0)
    def _():
        ext_l[0:CONV_PAD, :] = jnp.zeros((CONV_PAD, LRU_WIDTH), F32)
        ext_s[0:CONV_PAD, :] = jnp.zeros((CONV_PAD, SSD_CONV_DIM), F32)
        hcar[...] = jnp.zeros_like(hcar)
        sstate[...] = jnp.zeros_like(sstate)

    m = mod_ref[0]
    _modulate_rows(x_ref, h_s, g_ref[...], m[0:1], m[1:2], t)

    n_lru = LRU_WIDTH // LRU_COLS
    n_ssd = SSD_CONV_DIM // SSD_COLS

    def project_lru(p):
        cols = slice(LRU_COLS * p, LRU_COLS * (p + 1))
        ext_l[CONV_PAD:CONV_PAD + t, cols] = _dot(h_s[...], win_ref[:, s0 + LRU_COLS * p:s0 + LRU_COLS * (p + 1)])
        gl_s[:, cols] = _dot(h_s[...], win_ref[:, LRU_COLS * p:LRU_COLS * (p + 1)])

    def project_ssd(q):
        cols = slice(SSD_COLS * q, SSD_COLS * (q + 1))
        ext_s[CONV_PAD:CONV_PAD + t, cols] = _dot(h_s[...], win_ref[:, s2 + SSD_COLS * q:s2 + SSD_COLS * (q + 1)])

    project_lru(0)
    for p in range(n_lru):
        cols = slice(LRU_COLS * p, LRU_COLS * (p + 1))
        rcols = slice(2 * LRU_COLS * p, 2 * LRU_COLS * p + LRU_COLS)
        icols = slice(2 * LRU_COLS * p + LRU_COLS, 2 * LRU_COLS * (p + 1))
        if p + 1 < n_lru:
            project_lru(p + 1)
        else:
            project_ssd(0)
        for r in range(t // CONV_STRIP):
            xl_s[CONV_STRIP * r:CONV_STRIP * (r + 1), cols] = _conv_rows(ext_l, lcw_ref, lcb_ref, CONV_STRIP * r,
                                                                          CONV_STRIP, cols)
        ext_l[0:CONV_PAD, cols] = ext_l[t:t + CONV_PAD, cols]
        ri_s[:, 2 * LRU_COLS * p:2 * LRU_COLS * (p + 1)] = _dot(xl_s[:, cols].astype(BF16), wai_ref[p])
        sp = LRU_C * _softplus(-lam_ref[:, cols])
        ba = ba_ref[:, cols]
        bx = bx_ref[:, cols]
        carry = hcar[:, cols]
        for r in range(t // ROW_STRIP):
            rows = slice(ROW_STRIP * r, ROW_STRIP * (r + 1))
            rg = jax.nn.sigmoid(ri_s[rows, rcols] + ba)
            ig = jax.nn.sigmoid(ri_s[rows, icols] + bx)
            nlog_a = rg * sp
            a = jnp.exp(-nlog_a)
            u = jnp.sqrt(jnp.tanh(nlog_a) * (a * a + 1.0)) * (ig * xl_s[rows, cols])
            hs = []
            for q in range(ROW_STRIP // SUBLANE):
                a8, u8 = _lru_group_scan(a[SUBLANE * q:SUBLANE * (q + 1)], u[SUBLANE * q:SUBLANE * (q + 1)])
                h8 = a8 * carry + u8
                carry = h8[SUBLANE - 1:SUBLANE, :]
                hs.append(h8)
            ycat[rows, cols] = (jnp.concatenate(hs, axis=0) * _gelu_tanh(gl_s[rows, cols])).astype(BF16)
        hcar[:, cols] = carry

    for q in range(n_ssd):
        cols = slice(SSD_COLS * q, SSD_COLS * (q + 1))
        if q + 1 < n_ssd:
            project_ssd(q + 1)
        else:
            z_s[...] = _dot(h_s[...], win_ref[:, s1:s2])
            dt_s[...] = _softplus(_dot(h_s[...], wdt_ref[...]) + dtb_ref[...])
        for r in range(t // CONV_STRIP):
            xbc_s[CONV_STRIP * r:CONV_STRIP * (r + 1), cols] = _silu(
                _conv_rows(ext_s, scw_ref, scb_ref, CONV_STRIP * r, CONV_STRIP, cols))
        ext_s[0:CONV_PAD, cols] = ext_s[t:t + CONV_PAD, cols]
    y_lru = _dot(ycat[:, 0:LRU_WIDTH], wout_ref[0:LRU_WIDTH, :])

    ri_ = lax.broadcasted_iota(jnp.int32, (ln, ln), 0)
    ci_ = lax.broadcasted_iota(jnp.int32, (ln, ln), 1)
    causal = ci_ <= ri_
    tril = jnp.where(causal, 1.0, 0.0).astype(BF16)
    low_half = lax.broadcasted_iota(jnp.int32, (ln, LANE), 1) < SSD_HEADDIM
    for c in range(t // ln):
        _ssd_chunk(c, causal, tril, low_half, xbc_s, dt_s, z_s, sstate, ycat, alog_ref, dexp_ref, sng_ref, e_ref)

    y_mix = y_lru + _dot(ycat[:, LRU_WIDTH:], wout_ref[LRU_WIDTH:, :])
    o_ref[0] = x_ref[0] + (1.0 + m[2:3]) * y_mix


def _hyb_call(x, mod3, g, w_in_b, w_out_b, e_idx, w_dt, lru_conv_w, lru_conv_b, lru_wa, lru_ba, lru_wx, lru_bx,
              lru_lambda, ssd_conv_w, ssd_conv_b, ssd_dt_bias, ssd_a_log, ssd_d, ssd_norm_g):
    bsz, seq, d = x.shape
    t = HYB_TILE
    s3 = HYB_SPLITS[-1]
    w_dt_p = jnp.zeros((d, LANE), BF16).at[:, :SSD_HEADS].set(w_dt.astype(BF16))
    npair = LRU_HEADS // 2
    za = jnp.zeros((npair, LRU_BLOCK, LRU_BLOCK), F32)
    wa2 = lru_wa.reshape(npair, 2, LRU_BLOCK, LRU_BLOCK)
    wx2 = lru_wx.reshape(npair, 2, LRU_BLOCK, LRU_BLOCK)
    wai = jnp.concatenate([jnp.concatenate([wa2[:, 0], za, wx2[:, 0], za], axis=-1),
                           jnp.concatenate([za, wa2[:, 1], za, wx2[:, 1]], axis=-1)], axis=1).astype(BF16)
    expand = jnp.zeros((LANE, SSD_WIDTH), F32).at[:SSD_HEADS].set(
        jnp.repeat(jnp.eye(SSD_HEADS, dtype=F32), SSD_HEADDIM, axis=1)).astype(BF16)
    r = HYB_ROWS
    small = _pack_rows([(r["g"], g), (r["lcb"], lru_conv_b), (r["ba"], lru_ba), (r["bx"], lru_bx),
                        (r["lam"], lru_lambda), (r["dexp"], jnp.repeat(ssd_d, SSD_HEADDIM)), (r["sng"], ssd_norm_g),
                        (r["scb"], ssd_conv_b), (r["dtb"], ssd_dt_bias), (r["alog"], ssd_a_log),
                        (r["lcw"], lru_conv_w), (r["scw"], ssd_conv_w)], SSD_CONV_DIM)
    rest = (w_dt_p, wai, expand)
    in_specs = ([_tile_spec(t, d), _mod_spec(d), _const_spec(small.shape), _select_spec((d, s3), (e_idx,))]
                + [_const_spec(a.shape) for a in rest]
                + [_select_spec(w_out_b.shape[1:], (e_idx,))])
    scratch = [
        pltpu.VMEM((t, d), BF16),
        pltpu.VMEM((t + CONV_PAD, LRU_WIDTH), F32),
        pltpu.VMEM((t, LRU_WIDTH), F32),
        pltpu.VMEM((t, 2 * LRU_WIDTH), F32),
        pltpu.VMEM((t, LRU_WIDTH), F32),
        pltpu.VMEM((1, LRU_WIDTH), F32),
        pltpu.VMEM((t + CONV_PAD, SSD_CONV_DIM), F32),
        pltpu.VMEM((t, SSD_CONV_DIM), F32),
        pltpu.VMEM((t, LANE), F32),
        pltpu.VMEM((t, SSD_WIDTH), F32),
        pltpu.VMEM((SSD_STATE, SSD_WIDTH), F32),
        pltpu.VMEM((t, LRU_WIDTH + SSD_WIDTH), BF16),
    ]
    return pl.pallas_call(
        _hyb_kernel,
        grid=(bsz, seq // t),
        in_specs=in_specs,
        out_specs=_tile_spec(t, d),
        out_shape=jax.ShapeDtypeStruct((bsz, seq, d), F32),
        scratch_shapes=scratch,
        compiler_params=_params(("arbitrary", "arbitrary")),
        name="hybrid_mixer",
    )(x, mod3, small, w_in_b, *rest, w_out_b)


MLSTM_ROWS = dict(g=0, cb=1, bg=2, ng=3, skip=4, cw=8)


def _mlstm_kernel(x_ref, mod_ref, small_ref, wup_ref, wqk_ref, wv_ref, wg_ref, wdown_ref,
                  o_ref,
                  ext, q_s, k_s, v_s, cst, cst_b, nst, mst):
    ln = x_ref.shape[1]
    w = MLSTM_WIDTH
    dh = MLSTM_HEADDIM
    nblk = w // MXU_DIM

    def rows(name, n, width):
        return _RowView(small_ref, MLSTM_ROWS[name], n, width)

    g_ref, bg_ref = rows("g", 1, D_MODEL), rows("bg", 1, LANE)
    cb_ref, ng_ref, skip_ref, cw_ref = rows("cb", 1, w), rows("ng", 1, w), rows("skip", 1, w), rows("cw", CONV_WIDTH, w)

    @pl.when(pl.program_id(1) == 0)
    def _():
        ext[0:CONV_PAD, :] = jnp.zeros((CONV_PAD, w), F32)
        cst[...] = jnp.zeros_like(cst)
        cst_b[...] = jnp.zeros_like(cst_b)
        nst[...] = jnp.zeros_like(nst)
        mst[...] = jnp.zeros_like(mst)

    x = x_ref[0]
    m = mod_ref[0]
    h = _modulate(x, g_ref[...], m[0:1], m[1:2]).astype(BF16)
    xm = _dot(h, wup_ref[:, 0:w])
    z = _dot(h, wup_ref[:, w:2 * w])
    ext[CONV_PAD:CONV_PAD + ln, :] = xm
    xc = _silu(_conv_rows(ext, cw_ref, cb_ref, 0, ln, slice(0, w)))
    ext[0:CONV_PAD, :] = xm[ln - CONV_PAD:ln, :]
    xc_b = xc.astype(BF16)
    xm_b = xm.astype(BF16)
    for b in range(nblk):
        sl = slice(MXU_DIM * b, MXU_DIM * (b + 1))
        qk = _dot(xc_b[:, sl], wqk_ref[b])
        q_s[:, sl] = qk[:, :MXU_DIM]
        k_s[:, sl] = qk[:, MXU_DIM:]
        v_s[:, sl] = _dot(xm_b[:, sl], wv_ref[b])
    gates = (_dot(q_s[...].astype(BF16), wg_ref[0:w, :]) + _dot(k_s[...].astype(BF16), wg_ref[w:2 * w, :])
             + _dot(v_s[...].astype(BF16), wg_ref[2 * w:3 * w, :]) + bg_ref[...])
    log_f = -_softplus(-gates)
    ri_ = lax.broadcasted_iota(jnp.int32, (ln, ln), 0)
    ci_ = lax.broadcasted_iota(jnp.int32, (ln, ln), 1)
    causal = ci_ <= ri_
    tril = jnp.where(causal, 1.0, 0.0).astype(BF16)
    bcum = _dot_lhs01(tril, log_f, 3)
    gates_t = gates.T
    bcum_t = bcum.T
    k_scale = dh ** -0.5
    heads = range(MLSTM_HEADS)
    hsl = [slice(dh * hd, dh * (hd + 1)) for hd in heads]
    st = []
    for hd in heads:
        fi = MLSTM_HEADS + hd
        bc_col = bcum[:, fi:fi + 1]
        bc_row = bcum_t[fi:fi + 1, :]
        i_col = gates[:, hd:hd + 1]
        i_row = gates_t[hd:hd + 1, :]
        m_prev = mst[hd][:, 0:1]
        log_d = jnp.where(causal, bc_col - bc_row + i_row, -jnp.inf)
        g_col = bc_col + m_prev
        m_rows = jnp.maximum(g_col, jnp.max(log_d, axis=-1, keepdims=True))
        b_last = bc_col[ln - 1:ln, :]
        log_w = b_last - bc_col + i_col
        m_next = jnp.maximum(b_last + m_prev, jnp.max(log_w, axis=0, keepdims=True))
        st.append(dict(m_rows=m_rows, w_inter=jnp.exp(g_col - m_rows), d_mat=jnp.exp(log_d - m_rows),
                       w_s=jnp.exp(log_w - m_next), decay=jnp.exp(b_last + m_prev - m_next), m_next=m_next))
    for hd in heads:
        s = st[hd]
        q_h = q_s[:, hsl[hd]]
        k_h = k_s[:, hsl[hd]] * k_scale
        q_b = q_h.astype(BF16)
        v_b = v_s[:, hsl[hd]].astype(BF16)
        kw = k_h * s["w_s"]
        s.update(q_h=q_h, v_b=v_b, kw_sum=jnp.sum(kw, axis=0, keepdims=True),
                 qk=lax.dot_general(q_b, k_h.astype(BF16), (((1,), (1,)), ((), ())), preferred_element_type=F32),
                 qc=_dot(q_b, cst_b[hd]),
                 upd=lax.dot_general(kw.astype(BF16), v_b, (((0,), (0,)), ((), ())), preferred_element_type=F32))
    for hd in heads:
        s = st[hd]
        s_mat = s["qk"] * s["d_mat"]
        n_prev = nst[hd]
        num = s["w_inter"] * s["qc"] + _dot(s_mat.astype(BF16), s["v_b"])
        den = (s["w_inter"] * jnp.sum(s["q_h"] * n_prev, axis=-1, keepdims=True)
               + jnp.sum(s_mat, axis=-1, keepdims=True))
        s["h_out"] = num / jnp.maximum(jnp.abs(den), jnp.exp(-s["m_rows"]))
        c_new = s["decay"] * cst[hd] + s["upd"]
        cst[hd] = c_new
        cst_b[hd] = c_new.astype(BF16)
        nst[hd] = s["decay"] * n_prev + s["kw_sum"]
        mst[hd] = jnp.broadcast_to(s["m_next"], (1, LANE))
    y_mix = None
    for hd in heads:
        sl = hsl[hd]
        h_out = st[hd]["h_out"]
        mu = jnp.mean(h_out, axis=-1, keepdims=True)
        hc = h_out - mu
        var = jnp.mean(hc * hc, axis=-1, keepdims=True)
        hn = (hc * lax.rsqrt(var + EPS)) * ng_ref[:, sl]
        out = (hn + skip_ref[:, sl] * xc[:, sl]) * _silu(z[:, sl])
        y_head = _dot(out.astype(BF16), wdown_ref[sl, :])
        y_mix = y_head if y_mix is None else y_mix + y_head

    o_ref[0] = x + (1.0 + m[2:3]) * y_mix


def _expand_block_diag(w, group):
    nb, bi, bo = w.shape
    rows = w.reshape(nb // group, group * bi, bo)
    r_idx = jnp.arange(group * bi)[:, None]
    c_idx = jnp.arange(group * bo)[None, :]
    period = (c_idx % bo == jnp.arange(bo)[:, None]).astype(w.dtype)
    tiled = jnp.einsum('gro,oc->grc', rows, period, precision=lax.Precision.HIGHEST)
    return jnp.where((r_idx // bi == c_idx // bo)[None], tiled, 0.0)


def _mlstm_call(x, mod3, g, w_up_b, w_down_b, o_idx, conv_w, conv_b, wq, wk, wv, w_gates, b_gates, norm_g, skip):
    bsz, seq, d = x.shape
    ln = MLSTM_CHUNK
    w = MLSTM_WIDTH
    grp = MXU_DIM // MLSTM_QKV_BLOCK
    wqk = jnp.concatenate([_expand_block_diag(wq, grp), _expand_block_diag(wk, grp)], axis=-1).astype(BF16)
    wvd = _expand_block_diag(wv, grp).astype(BF16)
    wg_p = jnp.zeros((3 * w, LANE), BF16).at[:, :2 * MLSTM_HEADS].set(w_gates.astype(BF16))
    r = MLSTM_ROWS
    small = _pack_rows([(r["g"], g), (r["cb"], conv_b), (r["bg"], b_gates), (r["ng"], norm_g), (r["skip"], skip),
                        (r["cw"], conv_w)], w)
    mid = (wqk, wvd, wg_p)
    in_specs = ([_tile_spec(ln, d), _mod_spec(d), _const_spec(small.shape),
                 _select_spec(w_up_b.shape[1:], (o_idx,))]
                + [_const_spec(a.shape) for a in mid]
                + [_select_spec(w_down_b.shape[1:], (o_idx,))])
    scratch = [
        pltpu.VMEM((ln + CONV_PAD, w), F32),
        pltpu.VMEM((ln, w), F32),
        pltpu.VMEM((ln, w), F32),
        pltpu.VMEM((ln, w), F32),
        pltpu.VMEM((MLSTM_HEADS, MLSTM_HEADDIM, MLSTM_HEADDIM), F32),
        pltpu.VMEM((MLSTM_HEADS, MLSTM_HEADDIM, MLSTM_HEADDIM), BF16),
        pltpu.VMEM((MLSTM_HEADS, 1, MLSTM_HEADDIM), F32),
        pltpu.VMEM((MLSTM_HEADS, 1, LANE), F32),
    ]
    return pl.pallas_call(
        _mlstm_kernel,
        grid=(bsz, seq // ln),
        in_specs=in_specs,
        out_specs=_tile_spec(ln, d),
        out_shape=jax.ShapeDtypeStruct((bsz, seq, d), F32),
        scratch_shapes=scratch,
        compiler_params=_params(("arbitrary", "arbitrary")),
        name="mlstm_mixer",
    )(x, mod3, small, w_up_b, *mid, w_down_b)


def kernel(x, c, ada_w, ada_b, norm_g, ffn_w_gate, ffn_w_up, ffn_w_down, hyb_w_in, hyb_w_out, lru_conv_w, lru_conv_b, lru_wa, lru_ba, lru_wx, lru_bx, lru_lambda, ssd_conv_w, ssd_conv_b, ssd_dt_bias, ssd_a_log, ssd_d, ssd_norm_g, mlstm_w_up, mlstm_conv_w, mlstm_conv_b, mlstm_wq, mlstm_wk, mlstm_wv, mlstm_w_gates, mlstm_b_gates, mlstm_norm_g, mlstm_skip, mlstm_w_down, final_norm_g):
    depth = ada_w.shape[0]
    mod = _ada_call(c, ada_w, ada_b)
    wg_b, wu_b, wd_b = _cast_bf16(ffn_w_gate), _cast_bf16(ffn_w_up), _cast_bf16(ffn_w_down)
    hyb_in_b, hyb_out_b = _cast_bf16(hyb_w_in), _cast_bf16(hyb_w_out)
    up_b, down_b = _cast_bf16(mlstm_w_up), _cast_bf16(mlstm_w_down)
    for layer in range(depth):
        x = _ffn_call(x, mod[layer, :, 0], norm_g[layer, 0], wg_b, wu_b, wd_b, final_norm_g,
                      lead=(layer, 0), final_norm=False)
        if layer % 2 == 0:
            e = layer // 2
            x = _hyb_call(x, mod[layer, :, 1], norm_g[layer, 1], hyb_in_b, hyb_out_b, e,
                          hyb_w_in[e][:, HYB_SPLITS[-1]:], lru_conv_w[e], lru_conv_b[e], lru_wa[e], lru_ba[e],
                          lru_wx[e], lru_bx[e], lru_lambda[e], ssd_conv_w[e], ssd_conv_b[e], ssd_dt_bias[e],
                          ssd_a_log[e], ssd_d[e], ssd_norm_g[e])
        else:
            o = layer // 2
            x = _mlstm_call(x, mod[layer, :, 1], norm_g[layer, 1], up_b, down_b, o, mlstm_conv_w[o], mlstm_conv_b[o],
                            mlstm_wq[o], mlstm_wk[o], mlstm_wv[o], mlstm_w_gates[o], mlstm_b_gates[o],
                            mlstm_norm_g[o], mlstm_skip[o])
        x = _ffn_call(x, mod[layer, :, 2], norm_g[layer, 2], wg_b, wu_b, wd_b, final_norm_g,
                      lead=(layer, 1), final_norm=(layer == depth - 1))
    return x
```

```python
import functools
import math

import jax
import jax.numpy as jnp
from jax import lax
from jax.experimental import pallas as pl
from jax.experimental.pallas import tpu as pltpu

F32 = jnp.float32
BF16 = jnp.bfloat16

D_MODEL = 1024
N_SUB = 3
HALF_STEP = 0.5
D_FF = 2816
CONV_WIDTH = 4
EPS = 1e-6
LRU_WIDTH = D_MODEL
LRU_HEADS = 8
LRU_BLOCK = LRU_WIDTH // LRU_HEADS
LRU_C = 8.0
SSD_WIDTH = D_MODEL
SSD_HEADDIM = 64
SSD_HEADS = SSD_WIDTH // SSD_HEADDIM
SSD_GROUPS = 2
SSD_STATE = 128
SSD_CHUNK = 128
SSD_CONV_DIM = SSD_WIDTH + 2 * SSD_GROUPS * SSD_STATE
HYB_SPLITS = (LRU_WIDTH, 2 * LRU_WIDTH, 2 * LRU_WIDTH + SSD_WIDTH, 2 * LRU_WIDTH + SSD_WIDTH + SSD_CONV_DIM)
MLSTM_WIDTH = 2 * D_MODEL
MLSTM_HEADS = 4
MLSTM_HEADDIM = MLSTM_WIDTH // MLSTM_HEADS
MLSTM_QKV_BLOCK = 4

LANE = 128
SUBLANE = 8
PACKED_ROWS = 16
MXU_DIM = 256
VMEM_LIMIT_BYTES = 56 * 1024 * 1024

FFN_TILE = 1024
FFN_PART = 256
HYB_TILE = 512
MLSTM_CHUNK = 256
ADA_TILE_N = 1536
CAST_ROWS = 512
CONV_PAD = SUBLANE
ROW_STRIP = PACKED_ROWS
CONV_STRIP = 64
LRU_COLS = 2 * LRU_BLOCK
SSD_COLS = 512


def _silu(x):
    return x * jax.nn.sigmoid(x)


def _softplus(x):
    return jnp.maximum(x, 0.0) + jnp.log1p(jnp.exp(-jnp.abs(x)))


def _gelu_tanh(x):
    c = math.sqrt(2.0 / math.pi)
    return x * (0.5 * (1.0 + jnp.tanh(c * (x + 0.044715 * (x * x * x)))))


def _modulate(x, g, shift, scale):
    y = x * lax.rsqrt(jnp.mean(x * x, axis=-1, keepdims=True) + EPS)
    return (y * g) * (1.0 + scale) + shift


def _modulate_rows(x_ref, h_ref, g, shift, scale, t):
    for r in range(t // ROW_STRIP):
        rows = slice(ROW_STRIP * r, ROW_STRIP * (r + 1))
        h_ref[rows, :] = _modulate(x_ref[0, rows, :], g, shift, scale).astype(BF16)


def _split_bf16(x, n):
    parts = []
    r = x
    for _ in range(n):
        p = r.astype(BF16)
        parts.append(p)
        r = r - p.astype(F32)
    return parts


def _dot(a, b):
    return jnp.dot(a, b, preferred_element_type=F32)


def _dot_lhs01(m01, x, n):
    out = None
    for p in _split_bf16(x, n):
        t = _dot(m01, p)
        out = t if out is None else out + t
    return out


def _dot_rhs01(x, m01, n):
    out = None
    for p in _split_bf16(x, n):
        t = _dot(p, m01)
        out = t if out is None else out + t
    return out


def _conv_rows(ext_ref, w_ref, b_ref, r0, nrows, cols):
    ntile = nrows // SUBLANE
    tiles = [ext_ref[r0 + SUBLANE * i:r0 + SUBLANE * (i + 1), cols] for i in range(ntile + 1)]
    row = lax.broadcasted_iota(jnp.int32, tiles[0].shape, 0)
    bias = b_ref[:, cols]
    w_now = w_ref[CONV_WIDTH - 1:CONV_WIDTH, cols]
    acc = [bias + w_now * tiles[i + 1] for i in range(ntile)]
    for s in range(1, CONV_WIDTH):
        w_s = w_ref[CONV_WIDTH - 1 - s:CONV_WIDTH - s, cols]
        rot = [pltpu.roll(tl, s, 0) for tl in tiles]
        head = row < s
        acc = [acc[i] + w_s * jnp.where(head, rot[i], rot[i + 1]) for i in range(ntile)]
    return jnp.concatenate(acc, axis=0)


def _const_spec(shape):
    nd = len(shape)
    return pl.BlockSpec(shape, lambda *_: (0,) * nd, pipeline_mode=pl.Buffered(1))


def _select_spec(block, lead):
    nl = len(lead)
    return pl.BlockSpec((None,) * nl + tuple(block), lambda *_: tuple(lead) + (0,) * len(block),
                        pipeline_mode=pl.Buffered(1))


def _tile_spec(t, d):
    return pl.BlockSpec((1, t, d), lambda b, i: (b, i, 0))


def _mod_spec(d):
    return pl.BlockSpec((1, 3, d), lambda b, i: (b, 0, 0))


def _params(semantics):
    return pltpu.CompilerParams(dimension_semantics=semantics, vmem_limit_bytes=VMEM_LIMIT_BYTES)


def _cast_kernel(w_ref, o_ref):
    o_ref[...] = w_ref[...].astype(BF16)


def _cast_bf16(w):
    shape = w.shape
    cols = shape[-1]
    rows = math.prod(shape[:-1])
    out = pl.pallas_call(
        _cast_kernel,
        grid=(rows // CAST_ROWS,),
        in_specs=[pl.BlockSpec((CAST_ROWS, cols), lambda i: (i, 0))],
        out_specs=pl.BlockSpec((CAST_ROWS, cols), lambda i: (i, 0)),
        out_shape=jax.ShapeDtypeStruct((rows, cols), BF16),
        compiler_params=_params(("arbitrary",)),
        name="cast_bf16",
    )(w.reshape(rows, cols))
    return out.reshape(shape)


def _ada_kernel(c_ref, w_ref, b_ref, o_ref):
    c = c_ref[...]
    ca = _silu(c)
    w = w_ref[0]
    c1, c2 = _split_bf16(ca, 2)
    w1, w2 = _split_bf16(w, 2)
    o_ref[0] = _dot(c1, w1) + (_dot(c1, w2) + _dot(c2, w1)) + b_ref[0]


def _ada_call(c, ada_w, ada_b):
    depth, d, n = ada_w.shape
    bsz = c.shape[0]
    rows = -(-bsz // SUBLANE) * SUBLANE
    c_pad = jnp.zeros((rows, d), F32).at[:bsz].set(c)
    out = pl.pallas_call(
        _ada_kernel,
        grid=(depth, n // ADA_TILE_N),
        in_specs=[pl.BlockSpec((rows, d), lambda l, j: (0, 0)),
                  pl.BlockSpec((1, d, ADA_TILE_N), lambda l, j: (l, 0, j)),
                  pl.BlockSpec((1, 1, ADA_TILE_N), lambda l, j: (l, 0, j))],
        out_specs=pl.BlockSpec((1, rows, ADA_TILE_N), lambda l, j: (l, 0, j)),
        out_shape=jax.ShapeDtypeStruct((depth, rows, n), F32),
        compiler_params=_params(("arbitrary", "arbitrary")),
        name="ada_mod",
    )(c_pad, ada_w, ada_b.reshape(depth, 1, n))
    return out[:, :bsz].reshape(depth, bsz, N_SUB, 3, d)


def _ffn_kernel(x_ref, mod_ref, g_ref, wg_ref, wu_ref, wd_ref, fg_ref, o_ref, *, final_norm):
    m = mod_ref[0]
    t = x_ref.shape[1]
    n = t // FFN_PART
    parts = [slice(FFN_PART * p, FFN_PART * (p + 1)) for p in range(n)]

    def norm(p):
        return _modulate(x_ref[0, parts[p], :], g_ref[...], m[0:1], m[1:2]).astype(BF16)

    def project(h):
        return _dot(h, wg_ref[...]), _dot(h, wu_ref[...])

    h_next = norm(0)
    gu_next = project(h_next)
    h_next = norm(1) if n > 1 else None
    for p in range(n):
        gate, up = gu_next
        if p + 1 < n:
            gu_next = project(h_next)
            h_next = norm(p + 2) if p + 2 < n else None
        act = (_silu(gate) * up).astype(BF16)
        y = _dot(act, wd_ref[...])
        o = x_ref[0, parts[p], :] + (HALF_STEP * (1.0 + m[2:3])) * y
        if final_norm:
            o = (o * lax.rsqrt(jnp.mean(o * o, axis=-1, keepdims=True) + EPS)) * fg_ref[...]
        o_ref[0, parts[p], :] = o


def _ffn_call(x, mod3, g, wg, wu, wd, final_g, *, lead, final_norm):
    bsz, seq, d = x.shape
    f = wg.shape[-1]
    t = FFN_TILE
    return pl.pallas_call(
        functools.partial(_ffn_kernel, final_norm=final_norm),
        grid=(bsz, seq // t),
        in_specs=[_tile_spec(t, d), _mod_spec(d), _const_spec((1, d)),
                  _select_spec((d, f), lead), _select_spec((d, f), lead), _select_spec((f, d), lead),
                  _const_spec((1, d))],
        out_specs=_tile_spec(t, d),
        out_shape=jax.ShapeDtypeStruct((bsz, seq, d), F32),
        compiler_params=_params(("arbitrary", "arbitrary")),
        name="ffn",
    )(x, mod3, g.reshape(1, d), wg, wu, wd, final_g.reshape(1, d))


def _lru_group_scan(a, u):
    row = lax.broadcasted_iota(jnp.int32, a.shape, 0)
    for d in (1, 2, 4):
        a_prev = pltpu.roll(a, d, 0)
        u_prev = pltpu.roll(u, d, 0)
        keep = row >= d
        u = jnp.where(keep, a * u_prev + u, u)
        a = jnp.where(keep, a * a_prev, a)
    return a, u


def _ssd_chunk(c, causal, tril, low_half, xbc_s, dt_s, z_s, sstate, ycat, alog_ref, dexp_ref, sng_ref, e_ref):
    ln = SSD_CHUNK
    gn = SSD_GROUPS * SSD_STATE
    gw = SSD_WIDTH // SSD_GROUPS
    hpg = SSD_HEADS // SSD_GROUPS
    rows = slice(c * ln, (c + 1) * ln)
    xs = xbc_s[rows, 0:SSD_WIDTH]
    bm = xbc_s[rows, SSD_WIDTH:SSD_WIDTH + gn]
    cm = xbc_s[rows, SSD_WIDTH + gn:SSD_WIDTH + 2 * gn]
    dt = dt_s[rows, :]
    ac = dt * (-jnp.exp(alog_ref[...]))
    acs = _dot_lhs01(tril, ac, 3)
    acs_last = acs[ln - 1:ln, :]
    acs_t = acs.T
    dt_t = dt.T
    e = e_ref[...]
    dec_out = _dot_rhs01(jnp.exp(acs), e, 1)
    w_st = _dot_rhs01(dt * jnp.exp(acs_last - acs), e, 1)
    d_a = _dot_rhs01(jnp.broadcast_to(jnp.exp(acs_last), (SUBLANE, LANE)), e, 2)[0:1]
    xw_b = (xs * w_st).astype(BF16)
    s_prev = sstate[...]
    s_prev_b = s_prev.astype(BF16)
    ys = []
    sts = []
    for g in range(SSD_GROUPS):
        bg = bm[:, SSD_STATE * g:SSD_STATE * (g + 1)]
        cg_b = cm[:, SSD_STATE * g:SSD_STATE * (g + 1)].astype(BF16)
        bg_t = bg.T.astype(BF16)
        cb = _dot(cg_b, bg_t)
        y_off = _dot(cg_b, s_prev_b[:, gw * g:gw * (g + 1)])
        sts.append(_dot(bg_t, xw_b[:, gw * g:gw * (g + 1)]))
        for j in range(hpg // 2):
            pair = (hpg // 2) * g + j
            mats = []
            for hd in (2 * pair, 2 * pair + 1):
                diff = jnp.where(causal, acs[:, hd:hd + 1] - acs_t[hd:hd + 1, :], -jnp.inf)
                mats.append(cb * jnp.exp(diff) * dt_t[hd:hd + 1, :])
            lhs = jnp.concatenate(mats, axis=1).astype(BF16)
            xp = xs[:, LANE * pair:LANE * (pair + 1)]
            rhs = jnp.concatenate([jnp.where(low_half, xp, 0.0), jnp.where(low_half, 0.0, xp)],
                                  axis=0).astype(BF16)
            ys.append(_dot(lhs, rhs)
                      + y_off[:, LANE * j:LANE * (j + 1)] * dec_out[:, LANE * pair:LANE * (pair + 1)])
    sstate[...] = s_prev * d_a + jnp.concatenate(sts, axis=1)
    y = jnp.concatenate(ys, axis=1) + dexp_ref[...] * xs
    y = y * _silu(z_s[rows, :])
    for g in range(SSD_GROUPS):
        yg = y[:, gw * g:gw * (g + 1)]
        yn = (yg * lax.rsqrt(jnp.mean(yg * yg, axis=-1, keepdims=True) + EPS)) * sng_ref[:, gw * g:gw * (g + 1)]
        ycat[rows, LRU_WIDTH + gw * g:LRU_WIDTH + gw * (g + 1)] = yn.astype(BF16)


def _hyb_kernel(x_ref, mod_ref, g_ref, win_ref, wdt_ref, lcw_ref, lcb_ref, wai_ref, ba_ref, bx_ref, lam_ref,
                scw_ref, scb_ref, dtb_ref, alog_ref, dexp_ref, sng_ref, e_ref, wout_ref,
                o_ref,
                h_s, ext_l, xl_s, ri_s, gl_s, hcar, ext_s, xbc_s, dt_s, z_s, sstate, ycat):
    t = x_ref.shape[1]
    ln = SSD_CHUNK
    s0, s1, s2, s3 = HYB_SPLITS

    @pl.when(pl.program_id(1) == 0)
    def _():
        ext_l[0:CONV_PAD, :] = jnp.zeros((CONV_PAD, LRU_WIDTH), F32)
        ext_s[0:CONV_PAD, :] = jnp.zeros((CONV_PAD, SSD_CONV_DIM), F32)
        hcar[...] = jnp.zeros_like(hcar)
        sstate[...] = jnp.zeros_like(sstate)

    m = mod_ref[0]
    _modulate_rows(x_ref, h_s, g_ref[...], m[0:1], m[1:2], t)

    n_lru = LRU_WIDTH // LRU_COLS
    n_ssd = SSD_CONV_DIM // SSD_COLS

    def project_lru(p):
        cols = slice(LRU_COLS * p, LRU_COLS * (p + 1))
        ext_l[CONV_PAD:CONV_PAD + t, cols] = _dot(h_s[...], win_ref[:, s0 + LRU_COLS * p:s0 + LRU_COLS * (p + 1)])
        gl_s[:, cols] = _dot(h_s[...], win_ref[:, LRU_COLS * p:LRU_COLS * (p + 1)])

    def project_ssd(q):
        cols = slice(SSD_COLS * q, SSD_COLS * (q + 1))
        ext_s[CONV_PAD:CONV_PAD + t, cols] = _dot(h_s[...], win_ref[:, s2 + SSD_COLS * q:s2 + SSD_COLS * (q + 1)])

    project_lru(0)
    for p in range(n_lru):
        cols = slice(LRU_COLS * p, LRU_COLS * (p + 1))
        rcols = slice(2 * LRU_COLS * p, 2 * LRU_COLS * p + LRU_COLS)
        icols = slice(2 * LRU_COLS * p + LRU_COLS, 2 * LRU_COLS * (p + 1))
        if p + 1 < n_lru:
            project_lru(p + 1)
        else:
            project_ssd(0)
        for r in range(t // CONV_STRIP):
            xl_s[CONV_STRIP * r:CONV_STRIP * (r + 1), cols] = _conv_rows(ext_l, lcw_ref, lcb_ref, CONV_STRIP * r,
                                                                          CONV_STRIP, cols)
        ext_l[0:CONV_PAD, cols] = ext_l[t:t + CONV_PAD, cols]
        ri_s[:, 2 * LRU_COLS * p:2 * LRU_COLS * (p + 1)] = _dot(xl_s[:, cols].astype(BF16), wai_ref[p])
        sp = LRU_C * _softplus(-lam_ref[:, cols])
        ba = ba_ref[:, cols]
        bx = bx_ref[:, cols]
        carry = hcar[:, cols]
        for r in range(t // ROW_STRIP):
            rows = slice(ROW_STRIP * r, ROW_STRIP * (r + 1))
            rg = jax.nn.sigmoid(ri_s[rows, rcols] + ba)
            ig = jax.nn.sigmoid(ri_s[rows, icols] + bx)
            nlog_a = rg * sp
            a = jnp.exp(-nlog_a)
            u = jnp.sqrt(jnp.tanh(nlog_a) * (a * a + 1.0)) * (ig * xl_s[rows, cols])
            hs = []
            for q in range(ROW_STRIP // SUBLANE):
                a8, u8 = _lru_group_scan(a[SUBLANE * q:SUBLANE * (q + 1)], u[SUBLANE * q:SUBLANE * (q + 1)])
                h8 = a8 * carry + u8
                carry = h8[SUBLANE - 1:SUBLANE, :]
                hs.append(h8)
            ycat[rows, cols] = (jnp.concatenate(hs, axis=0) * _gelu_tanh(gl_s[rows, cols])).astype(BF16)
        hcar[:, cols] = carry

    for q in range(n_ssd):
        cols = slice(SSD_COLS * q, SSD_COLS * (q + 1))
        if q + 1 < n_ssd:
            project_ssd(q + 1)
        else:
            z_s[...] = _dot(h_s[...], win_ref[:, s1:s2])
            dt_s[...] = _softplus(_dot(h_s[...], wdt_ref[...]) + dtb_ref[...])
        for r in range(t // CONV_STRIP):
            xbc_s[CONV_STRIP * r:CONV_STRIP * (r + 1), cols] = _silu(
                _conv_rows(ext_s, scw_ref, scb_ref, CONV_STRIP * r, CONV_STRIP, cols))
        ext_s[0:CONV_PAD, cols] = ext_s[t:t + CONV_PAD, cols]
    y_lru = _dot(ycat[:, 0:LRU_WIDTH], wout_ref[0:LRU_WIDTH, :])

    ri_ = lax.broadcasted_iota(jnp.int32, (ln, ln), 0)
    ci_ = lax.broadcasted_iota(jnp.int32, (ln, ln), 1)
    causal = ci_ <= ri_
    tril = jnp.where(causal, 1.0, 0.0).astype(BF16)
    low_half = lax.broadcasted_iota(jnp.int32, (ln, LANE), 1) < SSD_HEADDIM
    for c in range(t // ln):
        _ssd_chunk(c, causal, tril, low_half, xbc_s, dt_s, z_s, sstate, ycat, alog_ref, dexp_ref, sng_ref, e_ref)

    y_mix = y_lru + _dot(ycat[:, LRU_WIDTH:], wout_ref[LRU_WIDTH:, :])
    o_ref[0] = x_ref[0] + (1.0 + m[2:3]) * y_mix


def _hyb_call(x, mod3, g, w_in_b, w_out_b, e_idx, w_dt, lru_conv_w, lru_conv_b, lru_wa, lru_ba, lru_wx, lru_bx,
              lru_lambda, ssd_conv_w, ssd_conv_b, ssd_dt_bias, ssd_a_log, ssd_d, ssd_norm_g):
    bsz, seq, d = x.shape
    t = HYB_TILE
    s3 = HYB_SPLITS[-1]
    w_dt_p = jnp.zeros((d, LANE), BF16).at[:, :SSD_HEADS].set(w_dt.astype(BF16))
    npair = LRU_HEADS // 2
    za = jnp.zeros((npair, LRU_BLOCK, LRU_BLOCK), F32)
    wa2 = lru_wa.reshape(npair, 2, LRU_BLOCK, LRU_BLOCK)
    wx2 = lru_wx.reshape(npair, 2, LRU_BLOCK, LRU_BLOCK)
    wai = jnp.concatenate([jnp.concatenate([wa2[:, 0], za, wx2[:, 0], za], axis=-1),
                           jnp.concatenate([za, wa2[:, 1], za, wx2[:, 1]], axis=-1)], axis=1).astype(BF16)
    pad_h = lambda v: jnp.zeros((1, LANE), F32).at[0, :SSD_HEADS].set(v)
    expand = jnp.zeros((LANE, SSD_WIDTH), F32).at[:SSD_HEADS].set(
        jnp.repeat(jnp.eye(SSD_HEADS, dtype=F32), SSD_HEADDIM, axis=1)).astype(BF16)
    row = lambda v: v.reshape(1, -1)
    small = (row(g),)
    rest = (w_dt_p, lru_conv_w, row(lru_conv_b), wai, row(lru_ba), row(lru_bx), row(lru_lambda),
            ssd_conv_w, row(ssd_conv_b), pad_h(ssd_dt_bias), pad_h(ssd_a_log),
            row(jnp.repeat(ssd_d, SSD_HEADDIM)), row(ssd_norm_g), expand)
    in_specs = ([_tile_spec(t, d), _mod_spec(d), _const_spec((1, d)), _select_spec((d, s3), (e_idx,))]
                + [_const_spec(a.shape) for a in rest]
                + [_select_spec(w_out_b.shape[1:], (e_idx,))])
    scratch = [
        pltpu.VMEM((t, d), BF16),
        pltpu.VMEM((t + CONV_PAD, LRU_WIDTH), F32),
        pltpu.VMEM((t, LRU_WIDTH), F32),
        pltpu.VMEM((t, 2 * LRU_WIDTH), F32),
        pltpu.VMEM((t, LRU_WIDTH), F32),
        pltpu.VMEM((1, LRU_WIDTH), F32),
        pltpu.VMEM((t + CONV_PAD, SSD_CONV_DIM), F32),
        pltpu.VMEM((t, SSD_CONV_DIM), F32),
        pltpu.VMEM((t, LANE), F32),
        pltpu.VMEM((t, SSD_WIDTH), F32),
        pltpu.VMEM((SSD_STATE, SSD_WIDTH), F32),
        pltpu.VMEM((t, LRU_WIDTH + SSD_WIDTH), BF16),
    ]
    return pl.pallas_call(
        _hyb_kernel,
        grid=(bsz, seq // t),
        in_specs=in_specs,
        out_specs=_tile_spec(t, d),
        out_shape=jax.ShapeDtypeStruct((bsz, seq, d), F32),
        scratch_shapes=scratch,
        compiler_params=_params(("arbitrary", "arbitrary")),
        name="hybrid_mixer",
    )(x, mod3, *small, w_in_b, *rest, w_out_b)


def _mlstm_kernel(x_ref, mod_ref, g_ref, wup_ref, cw_ref, cb_ref, wqk_ref, wv_ref, wg_ref, bg_ref,
                  ng_ref, skip_ref, wdown_ref,
                  o_ref,
                  ext, q_s, k_s, v_s, cst, cst_b, nst, mst):
    ln = x_ref.shape[1]
    w = MLSTM_WIDTH
    dh = MLSTM_HEADDIM
    nblk = w // MXU_DIM

    @pl.when(pl.program_id(1) == 0)
    def _():
        ext[0:CONV_PAD, :] = jnp.zeros((CONV_PAD, w), F32)
        cst[...] = jnp.zeros_like(cst)
        cst_b[...] = jnp.zeros_like(cst_b)
        nst[...] = jnp.zeros_like(nst)
        mst[...] = jnp.zeros_like(mst)

    x = x_ref[0]
    m = mod_ref[0]
    h = _modulate(x, g_ref[...], m[0:1], m[1:2]).astype(BF16)
    xm = _dot(h, wup_ref[:, 0:w])
    z = _dot(h, wup_ref[:, w:2 * w])
    ext[CONV_PAD:CONV_PAD + ln, :] = xm
    xc = _silu(_conv_rows(ext, cw_ref, cb_ref, 0, ln, slice(0, w)))
    ext[0:CONV_PAD, :] = xm[ln - CONV_PAD:ln, :]
    xc_b = xc.astype(BF16)
    xm_b = xm.astype(BF16)
    for b in range(nblk):
        sl = slice(MXU_DIM * b, MXU_DIM * (b + 1))
        qk = _dot(xc_b[:, sl], wqk_ref[b])
        q_s[:, sl] = qk[:, :MXU_DIM]
        k_s[:, sl] = qk[:, MXU_DIM:]
        v_s[:, sl] = _dot(xm_b[:, sl], wv_ref[b])
    gates = (_dot(q_s[...].astype(BF16), wg_ref[0:w, :]) + _dot(k_s[...].astype(BF16), wg_ref[w:2 * w, :])
             + _dot(v_s[...].astype(BF16), wg_ref[2 * w:3 * w, :]) + bg_ref[...])
    log_f = -_softplus(-gates)
    ri_ = lax.broadcasted_iota(jnp.int32, (ln, ln), 0)
    ci_ = lax.broadcasted_iota(jnp.int32, (ln, ln), 1)
    causal = ci_ <= ri_
    tril = jnp.where(causal, 1.0, 0.0).astype(BF16)
    bcum = _dot_lhs01(tril, log_f, 3)
    gates_t = gates.T
    bcum_t = bcum.T
    k_scale = dh ** -0.5
    heads = range(MLSTM_HEADS)
    hsl = [slice(dh * hd, dh * (hd + 1)) for hd in heads]
    st = []
    for hd in heads:
        fi = MLSTM_HEADS + hd
        bc_col = bcum[:, fi:fi + 1]
        bc_row = bcum_t[fi:fi + 1, :]
        i_col = gates[:, hd:hd + 1]
        i_row = gates_t[hd:hd + 1, :]
        m_prev = mst[hd][:, 0:1]
        log_d = jnp.where(causal, bc_col - bc_row + i_row, -jnp.inf)
        g_col = bc_col + m_prev
        m_rows = jnp.maximum(g_col, jnp.max(log_d, axis=-1, keepdims=True))
        b_last = bc_col[ln - 1:ln, :]
        log_w = b_last - bc_col + i_col
        m_next = jnp.maximum(b_last + m_prev, jnp.max(log_w, axis=0, keepdims=True))
        st.append(dict(m_rows=m_rows, w_inter=jnp.exp(g_col - m_rows), d_mat=jnp.exp(log_d - m_rows),
                       w_s=jnp.exp(log_w - m_next), decay=jnp.exp(b_last + m_prev - m_next), m_next=m_next))
    def stage2(hd):
        s = st[hd]
        q_h = q_s[:, hsl[hd]]
        k_h = k_s[:, hsl[hd]] * k_scale
        q_b = q_h.astype(BF16)
        v_b = v_s[:, hsl[hd]].astype(BF16)
        kw = k_h * s["w_s"]
        s.update(q_h=q_h, v_b=v_b, kw_sum=jnp.sum(kw, axis=0, keepdims=True),
                 qk=_dot(q_b, k_h.T.astype(BF16)),
                 qc=_dot(q_b, cst_b[hd]),
                 upd=lax.dot_general(kw.astype(BF16), v_b, (((0,), (0,)), ((), ())), preferred_element_type=F32))
    def stage3(hd):
        s = st[hd]
        s_mat = s["qk"] * s["d_mat"]
        n_prev = nst[hd]
        num = s["w_inter"] * s["qc"] + _dot(s_mat.astype(BF16), s["v_b"])
        den = (s["w_inter"] * jnp.sum(s["q_h"] * n_prev, axis=-1, keepdims=True)
               + jnp.sum(s_mat, axis=-1, keepdims=True))
        s["h_out"] = num / jnp.maximum(jnp.abs(den), jnp.exp(-s["m_rows"]))
        c_new = s["decay"] * cst[hd] + s["upd"]
        cst[hd] = c_new
        cst_b[hd] = c_new.astype(BF16)
        nst[hd] = s["decay"] * n_prev + s["kw_sum"]
        mst[hd] = jnp.broadcast_to(s["m_next"], (1, LANE))
    def stage4(hd):
        sl = hsl[hd]
        h_out = st[hd]["h_out"]
        mu = jnp.mean(h_out, axis=-1, keepdims=True)
        hc = h_out - mu
        var = jnp.mean(hc * hc, axis=-1, keepdims=True)
        hn = (hc * lax.rsqrt(var + EPS)) * ng_ref[:, sl]
        out = (hn + skip_ref[:, sl] * xc[:, sl]) * _silu(z[:, sl])
        return _dot(out.astype(BF16), wdown_ref[sl, :])

    y_mix = None
    for hd in heads:
        stage2(hd)
    for hd in heads:
        stage3(hd)
    for hd in heads:
        y_head = stage4(hd)
        y_mix = y_head if y_mix is None else y_mix + y_head

    o_ref[0] = x + (1.0 + m[2:3]) * y_mix


def _expand_block_diag(w, group):
    nb, bi, bo = w.shape
    rows = w.reshape(nb // group, group * bi, bo)
    r_idx = jnp.arange(group * bi)[:, None]
    c_idx = jnp.arange(group * bo)[None, :]
    period = (c_idx % bo == jnp.arange(bo)[:, None]).astype(w.dtype)
    tiled = jnp.einsum('gro,oc->grc', rows, period, precision=lax.Precision.HIGHEST)
    return jnp.where((r_idx // bi == c_idx // bo)[None], tiled, 0.0)


def _mlstm_call(x, mod3, g, w_up_b, w_down_b, o_idx, conv_w, conv_b, wq, wk, wv, w_gates, b_gates, norm_g, skip):
    bsz, seq, d = x.shape
    ln = MLSTM_CHUNK
    w = MLSTM_WIDTH
    grp = MXU_DIM // MLSTM_QKV_BLOCK
    wqk = jnp.concatenate([_expand_block_diag(wq, grp), _expand_block_diag(wk, grp)], axis=-1).astype(BF16)
    wvd = _expand_block_diag(wv, grp).astype(BF16)
    wg_p = jnp.zeros((3 * w, LANE), BF16).at[:, :2 * MLSTM_HEADS].set(w_gates.astype(BF16))
    bg_p = jnp.zeros((1, LANE), F32).at[0, :2 * MLSTM_HEADS].set(b_gates)
    row = lambda v: v.reshape(1, -1)
    mid = (conv_w, row(conv_b), wqk, wvd, wg_p, bg_p, row(norm_g), row(skip))
    in_specs = ([_tile_spec(ln, d), _mod_spec(d), _const_spec((1, d)), _select_spec(w_up_b.shape[1:], (o_idx,))]
                + [_const_spec(a.shape) for a in mid]
                + [_select_spec(w_down_b.shape[1:], (o_idx,))])
    scratch = [
        pltpu.VMEM((ln + CONV_PAD, w), F32),
        pltpu.VMEM((ln, w), F32),
        pltpu.VMEM((ln, w), F32),
        pltpu.VMEM((ln, w), F32),
        pltpu.VMEM((MLSTM_HEADS, MLSTM_HEADDIM, MLSTM_HEADDIM), F32),
        pltpu.VMEM((MLSTM_HEADS, MLSTM_HEADDIM, MLSTM_HEADDIM), BF16),
        pltpu.VMEM((MLSTM_HEADS, 1, MLSTM_HEADDIM), F32),
        pltpu.VMEM((MLSTM_HEADS, 1, LANE), F32),
    ]
    return pl.pallas_call(
        _mlstm_kernel,
        grid=(bsz, seq // ln),
        in_specs=in_specs,
        out_specs=_tile_spec(ln, d),
        out_shape=jax.ShapeDtypeStruct((bsz, seq, d), F32),
        scratch_shapes=scratch,
        compiler_params=_params(("arbitrary", "arbitrary")),
        name="mlstm_mixer",
    )(x, mod3, row(g), w_up_b, *mid, w_down_b)


def kernel(x, c, ada_w, ada_b, norm_g, ffn_w_gate, ffn_w_up, ffn_w_down, hyb_w_in, hyb_w_out, lru_conv_w, lru_conv_b, lru_wa, lru_ba, lru_wx, lru_bx, lru_lambda, ssd_conv_w, ssd_conv_b, ssd_dt_bias, ssd_a_log, ssd_d, ssd_norm_g, mlstm_w_up, mlstm_conv_w, mlstm_conv_b, mlstm_wq, mlstm_wk, mlstm_wv, mlstm_w_gates, mlstm_b_gates, mlstm_norm_g, mlstm_skip, mlstm_w_down, final_norm_g):
    depth = ada_w.shape[0]
    mod = _ada_call(c, ada_w, ada_b)
    wg_b, wu_b, wd_b = _cast_bf16(ffn_w_gate), _cast_bf16(ffn_w_up), _cast_bf16(ffn_w_down)
    hyb_in_b, hyb_out_b = _cast_bf16(hyb_w_in), _cast_bf16(hyb_w_out)
    up_b, down_b = _cast_bf16(mlstm_w_up), _cast_bf16(mlstm_w_down)
    for layer in range(depth):
        x = _ffn_call(x, mod[layer, :, 0], norm_g[layer, 0], wg_b, wu_b, wd_b, final_norm_g,
                      lead=(layer, 0), final_norm=False)
        if layer % 2 == 0:
            e = layer // 2
            x = _hyb_call(x, mod[layer, :, 1], norm_g[layer, 1], hyb_in_b, hyb_out_b, e,
                          hyb_w_in[e][:, HYB_SPLITS[-1]:], lru_conv_w[e], lru_conv_b[e], lru_wa[e], lru_ba[e],
                          lru_wx[e], lru_bx[e], lru_lambda[e], ssd_conv_w[e], ssd_conv_b[e], ssd_dt_bias[e],
                          ssd_a_log[e], ssd_d[e], ssd_norm_g[e])
        else:
            o = layer // 2
            x = _mlstm_call(x, mod[layer, :, 1], norm_g[layer, 1], up_b, down_b, o, mlstm_conv_w[o], mlstm_conv_b[o],
                            mlstm_wq[o], mlstm_wk[o], mlstm_wv[o], mlstm_w_gates[o], mlstm_b_gates[o],
                            mlstm_norm_g[o], mlstm_skip[o])
        x = _ffn_call(x, mod[layer, :, 2], norm_g[layer, 2], wg_b, wu_b, wd_b, final_norm_g,
                      lead=(layer, 1), final_norm=(layer == depth - 1))
    return x
```

```python
import functools
import math

import jax
import jax.numpy as jnp
from jax import lax
from jax.experimental import pallas as pl
from jax.experimental.pallas import tpu as pltpu

F32 = jnp.float32
BF16 = jnp.bfloat16

D_MODEL = 1024
N_SUB = 3
HALF_STEP = 0.5
D_FF = 2816
CONV_WIDTH = 4
EPS = 1e-6
LRU_WIDTH = D_MODEL
LRU_HEADS = 8
LRU_BLOCK = LRU_WIDTH // LRU_HEADS
LRU_C = 8.0
SSD_WIDTH = D_MODEL
SSD_HEADDIM = 64
SSD_HEADS = SSD_WIDTH // SSD_HEADDIM
SSD_GROUPS = 2
SSD_STATE = 128
SSD_CHUNK = 128
SSD_CONV_DIM = SSD_WIDTH + 2 * SSD_GROUPS * SSD_STATE
HYB_SPLITS = (LRU_WIDTH, 2 * LRU_WIDTH, 2 * LRU_WIDTH + SSD_WIDTH, 2 * LRU_WIDTH + SSD_WIDTH + SSD_CONV_DIM)
MLSTM_WIDTH = 2 * D_MODEL
MLSTM_HEADS = 4
MLSTM_HEADDIM = MLSTM_WIDTH // MLSTM_HEADS
MLSTM_QKV_BLOCK = 4

LANE = 128
SUBLANE = 8
PACKED_ROWS = 16
MXU_DIM = 256
VMEM_LIMIT_BYTES = 56 * 1024 * 1024

FFN_TILE = 1024
FFN_PART = 256
HYB_TILE = 512
MLSTM_CHUNK = 256
ADA_TILE_N = 1536
CAST_ROWS = 512
CONV_PAD = SUBLANE
ROW_STRIP = PACKED_ROWS
CONV_STRIP = 64
LRU_COLS = 2 * LRU_BLOCK
SSD_COLS = 512


def _silu(x):
    return x * jax.nn.sigmoid(x)


def _softplus(x):
    return jnp.maximum(x, 0.0) + jnp.log1p(jnp.exp(-jnp.abs(x)))


def _gelu_tanh(x):
    c = math.sqrt(2.0 / math.pi)
    return x * (0.5 * (1.0 + jnp.tanh(c * (x + 0.044715 * (x * x * x)))))


def _modulate(x, g, shift, scale):
    y = x * lax.rsqrt(jnp.mean(x * x, axis=-1, keepdims=True) + EPS)
    return (y * g) * (1.0 + scale) + shift


def _modulate_rows(x_ref, h_ref, g, shift, scale, t):
    for r in range(t // ROW_STRIP):
        rows = slice(ROW_STRIP * r, ROW_STRIP * (r + 1))
        h_ref[rows, :] = _modulate(x_ref[0, rows, :], g, shift, scale).astype(BF16)


def _split_bf16(x, n):
    parts = []
    r = x
    for _ in range(n):
        p = r.astype(BF16)
        parts.append(p)
        r = r - p.astype(F32)
    return parts


def _dot(a, b):
    return jnp.dot(a, b, preferred_element_type=F32)


def _dot_lhs01(m01, x, n):
    out = None
    for p in _split_bf16(x, n):
        t = _dot(m01, p)
        out = t if out is None else out + t
    return out


def _dot_rhs01(x, m01, n):
    out = None
    for p in _split_bf16(x, n):
        t = _dot(p, m01)
        out = t if out is None else out + t
    return out


def _conv_rows(ext_ref, w_ref, b_ref, r0, nrows, cols):
    ntile = nrows // SUBLANE
    tiles = [ext_ref[r0 + SUBLANE * i:r0 + SUBLANE * (i + 1), cols] for i in range(ntile + 1)]
    row = lax.broadcasted_iota(jnp.int32, tiles[0].shape, 0)
    bias = b_ref[:, cols]
    w_now = w_ref[CONV_WIDTH - 1:CONV_WIDTH, cols]
    acc = [bias + w_now * tiles[i + 1] for i in range(ntile)]
    for s in range(1, CONV_WIDTH):
        w_s = w_ref[CONV_WIDTH - 1 - s:CONV_WIDTH - s, cols]
        rot = [pltpu.roll(tl, s, 0) for tl in tiles]
        head = row < s
        acc = [acc[i] + w_s * jnp.where(head, rot[i], rot[i + 1]) for i in range(ntile)]
    return jnp.concatenate(acc, axis=0)


def _const_spec(shape):
    nd = len(shape)
    return pl.BlockSpec(shape, lambda *_: (0,) * nd, pipeline_mode=pl.Buffered(1))


def _select_spec(block, lead):
    nl = len(lead)
    return pl.BlockSpec((None,) * nl + tuple(block), lambda *_: tuple(lead) + (0,) * len(block),
                        pipeline_mode=pl.Buffered(1))


def _tile_spec(t, d):
    return pl.BlockSpec((1, t, d), lambda b, i: (b, i, 0))


def _mod_spec(d):
    return pl.BlockSpec((1, 3, d), lambda b, i: (b, 0, 0))


def _params(semantics):
    return pltpu.CompilerParams(dimension_semantics=semantics, vmem_limit_bytes=VMEM_LIMIT_BYTES)


def _cast_kernel(w_ref, o_ref):
    o_ref[...] = w_ref[...].astype(BF16)


def _cast_bf16(w):
    shape = w.shape
    cols = shape[-1]
    rows = math.prod(shape[:-1])
    out = pl.pallas_call(
        _cast_kernel,
        grid=(rows // CAST_ROWS,),
        in_specs=[pl.BlockSpec((CAST_ROWS, cols), lambda i: (i, 0))],
        out_specs=pl.BlockSpec((CAST_ROWS, cols), lambda i: (i, 0)),
        out_shape=jax.ShapeDtypeStruct((rows, cols), BF16),
        compiler_params=_params(("arbitrary",)),
        name="cast_bf16",
    )(w.reshape(rows, cols))
    return out.reshape(shape)


def _ada_kernel(c_ref, w_ref, b_ref, o_ref):
    c = c_ref[...]
    ca = _silu(c)
    w = w_ref[0]
    c1, c2 = _split_bf16(ca, 2)
    w1, w2 = _split_bf16(w, 2)
    o_ref[0] = _dot(c1, w1) + (_dot(c1, w2) + _dot(c2, w1)) + b_ref[0]


def _ada_call(c, ada_w, ada_b):
    depth, d, n = ada_w.shape
    bsz = c.shape[0]
    rows = -(-bsz // SUBLANE) * SUBLANE
    c_pad = jnp.zeros((rows, d), F32).at[:bsz].set(c)
    out = pl.pallas_call(
        _ada_kernel,
        grid=(depth, n // ADA_TILE_N),
        in_specs=[pl.BlockSpec((rows, d), lambda l, j: (0, 0)),
                  pl.BlockSpec((1, d, ADA_TILE_N), lambda l, j: (l, 0, j)),
                  pl.BlockSpec((1, 1, ADA_TILE_N), lambda l, j: (l, 0, j))],
        out_specs=pl.BlockSpec((1, rows, ADA_TILE_N), lambda l, j: (l, 0, j)),
        out_shape=jax.ShapeDtypeStruct((depth, rows, n), F32),
        compiler_params=_params(("arbitrary", "arbitrary")),
        name="ada_mod",
    )(c_pad, ada_w, ada_b.reshape(depth, 1, n))
    return out[:, :bsz].reshape(depth, bsz, N_SUB, 3, d)


def _ffn_kernel(x_ref, mod_ref, g_ref, wg_ref, wu_ref, wd_ref, fg_ref, o_ref, *, final_norm):
    m = mod_ref[0]
    t = x_ref.shape[1]
    n = t // FFN_PART
    parts = [slice(FFN_PART * p, FFN_PART * (p + 1)) for p in range(n)]

    def norm(p):
        return _modulate(x_ref[0, parts[p], :], g_ref[...], m[0:1], m[1:2]).astype(BF16)

    def project(h):
        return _dot(h, wg_ref[...]), _dot(h, wu_ref[...])

    h_next = norm(0)
    gu_next = project(h_next)
    h_next = norm(1) if n > 1 else None
    for p in range(n):
        gate, up = gu_next
        if p + 1 < n:
            gu_next = project(h_next)
            h_next = norm(p + 2) if p + 2 < n else None
        act = (_silu(gate) * up).astype(BF16)
        y = _dot(act, wd_ref[...])
        o = x_ref[0, parts[p], :] + (HALF_STEP * (1.0 + m[2:3])) * y
        if final_norm:
            o = (o * lax.rsqrt(jnp.mean(o * o, axis=-1, keepdims=True) + EPS)) * fg_ref[...]
        o_ref[0, parts[p], :] = o


def _ffn_call(x, mod3, g, wg, wu, wd, final_g, *, lead, final_norm):
    bsz, seq, d = x.shape
    f = wg.shape[-1]
    t = FFN_TILE
    return pl.pallas_call(
        functools.partial(_ffn_kernel, final_norm=final_norm),
        grid=(bsz, seq // t),
        in_specs=[_tile_spec(t, d), _mod_spec(d), _const_spec((1, d)),
                  _select_spec((d, f), lead), _select_spec((d, f), lead), _select_spec((f, d), lead),
                  _const_spec((1, d))],
        out_specs=_tile_spec(t, d),
        out_shape=jax.ShapeDtypeStruct((bsz, seq, d), F32),
        compiler_params=_params(("arbitrary", "arbitrary")),
        name="ffn",
    )(x, mod3, g.reshape(1, d), wg, wu, wd, final_g.reshape(1, d))


def _lru_group_scan(a, u):
    row = lax.broadcasted_iota(jnp.int32, a.shape, 0)
    for d in (1, 2, 4):
        a_prev = pltpu.roll(a, d, 0)
        u_prev = pltpu.roll(u, d, 0)
        keep = row >= d
        u = jnp.where(keep, a * u_prev + u, u)
        a = jnp.where(keep, a * a_prev, a)
    return a, u


def _ssd_chunk(c, causal, tril, low_half, xbc_s, dt_s, z_s, sstate, ycat, alog_ref, dexp_ref, sng_ref, e_ref):
    ln = SSD_CHUNK
    gn = SSD_GROUPS * SSD_STATE
    gw = SSD_WIDTH // SSD_GROUPS
    hpg = SSD_HEADS // SSD_GROUPS
    rows = slice(c * ln, (c + 1) * ln)
    xs = xbc_s[rows, 0:SSD_WIDTH]
    bm = xbc_s[rows, SSD_WIDTH:SSD_WIDTH + gn]
    cm = xbc_s[rows, SSD_WIDTH + gn:SSD_WIDTH + 2 * gn]
    dt = dt_s[rows, :]
    ac = dt * (-jnp.exp(alog_ref[...]))
    acs = _dot_lhs01(tril, ac, 3)
    acs_last = acs[ln - 1:ln, :]
    acs_t = acs.T
    dt_t = dt.T
    e = e_ref[...]
    dec_out = _dot_rhs01(jnp.exp(acs), e, 1)
    w_st = _dot_rhs01(dt * jnp.exp(acs_last - acs), e, 1)
    d_a = _dot_rhs01(jnp.broadcast_to(jnp.exp(acs_last), (SUBLANE, LANE)), e, 2)[0:1]
    xw_b = (xs * w_st).astype(BF16)
    s_prev = sstate[...]
    s_prev_b = s_prev.astype(BF16)
    ys = []
    sts = []
    for g in range(SSD_GROUPS):
        bg = bm[:, SSD_STATE * g:SSD_STATE * (g + 1)]
        cg_b = cm[:, SSD_STATE * g:SSD_STATE * (g + 1)].astype(BF16)
        bg_t = bg.T.astype(BF16)
        cb = _dot(cg_b, bg_t)
        y_off = _dot(cg_b, s_prev_b[:, gw * g:gw * (g + 1)])
        sts.append(_dot(bg_t, xw_b[:, gw * g:gw * (g + 1)]))
        for j in range(hpg // 2):
            pair = (hpg // 2) * g + j
            mats = []
            for hd in (2 * pair, 2 * pair + 1):
                diff = jnp.where(causal, acs[:, hd:hd + 1] - acs_t[hd:hd + 1, :], -jnp.inf)
                mats.append(cb * jnp.exp(diff) * dt_t[hd:hd + 1, :])
            lhs = jnp.concatenate(mats, axis=1).astype(BF16)
            xp = xs[:, LANE * pair:LANE * (pair + 1)]
            rhs = jnp.concatenate([jnp.where(low_half, xp, 0.0), jnp.where(low_half, 0.0, xp)],
                                  axis=0).astype(BF16)
            ys.append(_dot(lhs, rhs)
                      + y_off[:, LANE * j:LANE * (j + 1)] * dec_out[:, LANE * pair:LANE * (pair + 1)])
    sstate[...] = s_prev * d_a + jnp.concatenate(sts, axis=1)
    y = jnp.concatenate(ys, axis=1) + dexp_ref[...] * xs
    y = y * _silu(z_s[rows, :])
    for g in range(SSD_GROUPS):
        yg = y[:, gw * g:gw * (g + 1)]
        yn = (yg * lax.rsqrt(jnp.mean(yg * yg, axis=-1, keepdims=True) + EPS)) * sng_ref[:, gw * g:gw * (g + 1)]
        ycat[rows, LRU_WIDTH + gw * g:LRU_WIDTH + gw * (g + 1)] = yn.astype(BF16)


def _hyb_kernel(x_ref, mod_ref, g_ref, win_ref, wdt_ref, lcw_ref, lcb_ref, wai_ref, ba_ref, bx_ref, lam_ref,
                scw_ref, scb_ref, dtb_ref, alog_ref, dexp_ref, sng_ref, e_ref, wout_ref,
                o_ref,
                h_s, ext_l, xl_s, ri_s, gl_s, hcar, ext_s, xbc_s, dt_s, z_s, sstate, ycat):
    t = x_ref.shape[1]
    ln = SSD_CHUNK
    s0, s1, s2, s3 = HYB_SPLITS

    @pl.when(pl.program_id(1) == 0)
    def _():
        ext_l[0:CONV_PAD, :] = jnp.zeros((CONV_PAD, LRU_WIDTH), F32)
        ext_s[0:CONV_PAD, :] = jnp.zeros((CONV_PAD, SSD_CONV_DIM), F32)
        hcar[...] = jnp.zeros_like(hcar)
        sstate[...] = jnp.zeros_like(sstate)

    m = mod_ref[0]
    _modulate_rows(x_ref, h_s, g_ref[...], m[0:1], m[1:2], t)

    n_lru = LRU_WIDTH // LRU_COLS
    n_ssd = SSD_CONV_DIM // SSD_COLS

    def project_lru(p):
        cols = slice(LRU_COLS * p, LRU_COLS * (p + 1))
        ext_l[CONV_PAD:CONV_PAD + t, cols] = _dot(h_s[...], win_ref[:, s0 + LRU_COLS * p:s0 + LRU_COLS * (p + 1)])
        gl_s[:, cols] = _dot(h_s[...], win_ref[:, LRU_COLS * p:LRU_COLS * (p + 1)])

    def project_ssd(q):
        cols = slice(SSD_COLS * q, SSD_COLS * (q + 1))
        ext_s[CONV_PAD:CONV_PAD + t, cols] = _dot(h_s[...], win_ref[:, s2 + SSD_COLS * q:s2 + SSD_COLS * (q + 1)])

    project_lru(0)
    for p in range(n_lru):
        cols = slice(LRU_COLS * p, LRU_COLS * (p + 1))
        rcols = slice(2 * LRU_COLS * p, 2 * LRU_COLS * p + LRU_COLS)
        icols = slice(2 * LRU_COLS * p + LRU_COLS, 2 * LRU_COLS * (p + 1))
        if p + 1 < n_lru:
            project_lru(p + 1)
        else:
            project_ssd(0)
        for r in range(t // CONV_STRIP):
            xl_s[CONV_STRIP * r:CONV_STRIP * (r + 1), cols] = _conv_rows(ext_l, lcw_ref, lcb_ref, CONV_STRIP * r,
                                                                          CONV_STRIP, cols)
        ext_l[0:CONV_PAD, cols] = ext_l[t:t + CONV_PAD, cols]
        ri_s[:, 2 * LRU_COLS * p:2 * LRU_COLS * (p + 1)] = _dot(xl_s[:, cols].astype(BF16), wai_ref[p])
        sp = LRU_C * _softplus(-lam_ref[:, cols])
        ba = ba_ref[:, cols]
        bx = bx_ref[:, cols]
        carry = hcar[:, cols]
        for r in range(t // ROW_STRIP):
            rows = slice(ROW_STRIP * r, ROW_STRIP * (r + 1))
            rg = jax.nn.sigmoid(ri_s[rows, rcols] + ba)
            ig = jax.nn.sigmoid(ri_s[rows, icols] + bx)
            nlog_a = rg * sp
            a = jnp.exp(-nlog_a)
            u = jnp.sqrt(jnp.tanh(nlog_a) * (a * a + 1.0)) * (ig * xl_s[rows, cols])
            hs = []
            for q in range(ROW_STRIP // SUBLANE):
                a8, u8 = _lru_group_scan(a[SUBLANE * q:SUBLANE * (q + 1)], u[SUBLANE * q:SUBLANE * (q + 1)])
                h8 = a8 * carry + u8
                carry = h8[SUBLANE - 1:SUBLANE, :]
                hs.append(h8)
            ycat[rows, cols] = (jnp.concatenate(hs, axis=0) * _gelu_tanh(gl_s[rows, cols])).astype(BF16)
        hcar[:, cols] = carry

    for q in range(n_ssd):
        cols = slice(SSD_COLS * q, SSD_COLS * (q + 1))
        if q + 1 < n_ssd:
            project_ssd(q + 1)
        else:
            z_s[...] = _dot(h_s[...], win_ref[:, s1:s2])
            dt_s[...] = _softplus(_dot(h_s[...], wdt_ref[...]) + dtb_ref[...])
        for r in range(t // CONV_STRIP):
            xbc_s[CONV_STRIP * r:CONV_STRIP * (r + 1), cols] = _silu(
                _conv_rows(ext_s, scw_ref, scb_ref, CONV_STRIP * r, CONV_STRIP, cols))
        ext_s[0:CONV_PAD, cols] = ext_s[t:t + CONV_PAD, cols]
    y_lru = _dot(ycat[:, 0:LRU_WIDTH], wout_ref[0:LRU_WIDTH, :])

    ri_ = lax.broadcasted_iota(jnp.int32, (ln, ln), 0)
    ci_ = lax.broadcasted_iota(jnp.int32, (ln, ln), 1)
    causal = ci_ <= ri_
    tril = jnp.where(causal, 1.0, 0.0).astype(BF16)
    low_half = lax.broadcasted_iota(jnp.int32, (ln, LANE), 1) < SSD_HEADDIM
    for c in range(t // ln):
        _ssd_chunk(c, causal, tril, low_half, xbc_s, dt_s, z_s, sstate, ycat, alog_ref, dexp_ref, sng_ref, e_ref)

    y_mix = y_lru + _dot(ycat[:, LRU_WIDTH:], wout_ref[LRU_WIDTH:, :])
    o_ref[0] = x_ref[0] + (1.0 + m[2:3]) * y_mix


def _hyb_call(x, mod3, g, w_in_b, w_out_b, e_idx, w_dt, lru_conv_w, lru_conv_b, lru_wa, lru_ba, lru_wx, lru_bx,
              lru_lambda, ssd_conv_w, ssd_conv_b, ssd_dt_bias, ssd_a_log, ssd_d, ssd_norm_g):
    bsz, seq, d = x.shape
    t = HYB_TILE
    s3 = HYB_SPLITS[-1]
    w_dt_p = jnp.zeros((d, LANE), BF16).at[:, :SSD_HEADS].set(w_dt.astype(BF16))
    npair = LRU_HEADS // 2
    za = jnp.zeros((npair, LRU_BLOCK, LRU_BLOCK), F32)
    wa2 = lru_wa.reshape(npair, 2, LRU_BLOCK, LRU_BLOCK)
    wx2 = lru_wx.reshape(npair, 2, LRU_BLOCK, LRU_BLOCK)
    wai = jnp.concatenate([jnp.concatenate([wa2[:, 0], za, wx2[:, 0], za], axis=-1),
                           jnp.concatenate([za, wa2[:, 1], za, wx2[:, 1]], axis=-1)], axis=1).astype(BF16)
    pad_h = lambda v: jnp.zeros((1, LANE), F32).at[0, :SSD_HEADS].set(v)
    expand = jnp.zeros((LANE, SSD_WIDTH), F32).at[:SSD_HEADS].set(
        jnp.repeat(jnp.eye(SSD_HEADS, dtype=F32), SSD_HEADDIM, axis=1)).astype(BF16)
    row = lambda v: v.reshape(1, -1)
    small = (row(g),)
    rest = (w_dt_p, lru_conv_w, row(lru_conv_b), wai, row(lru_ba), row(lru_bx), row(lru_lambda),
            ssd_conv_w, row(ssd_conv_b), pad_h(ssd_dt_bias), pad_h(ssd_a_log),
            row(jnp.repeat(ssd_d, SSD_HEADDIM)), row(ssd_norm_g), expand)
    in_specs = ([_tile_spec(t, d), _mod_spec(d), _const_spec((1, d)), _select_spec((d, s3), (e_idx,))]
                + [_const_spec(a.shape) for a in rest]
                + [_select_spec(w_out_b.shape[1:], (e_idx,))])
    scratch = [
        pltpu.VMEM((t, d), BF16),
        pltpu.VMEM((t + CONV_PAD, LRU_WIDTH), F32),
        pltpu.VMEM((t, LRU_WIDTH), F32),
        pltpu.VMEM((t, 2 * LRU_WIDTH), F32),
        pltpu.VMEM((t, LRU_WIDTH), F32),
        pltpu.VMEM((1, LRU_WIDTH), F32),
        pltpu.VMEM((t + CONV_PAD, SSD_CONV_DIM), F32),
        pltpu.VMEM((t, SSD_CONV_DIM), F32),
        pltpu.VMEM((t, LANE), F32),
        pltpu.VMEM((t, SSD_WIDTH), F32),
        pltpu.VMEM((SSD_STATE, SSD_WIDTH), F32),
        pltpu.VMEM((t, LRU_WIDTH + SSD_WIDTH), BF16),
    ]
    return pl.pallas_call(
        _hyb_kernel,
        grid=(bsz, seq // t),
        in_specs=in_specs,
        out_specs=_tile_spec(t, d),
        out_shape=jax.ShapeDtypeStruct((bsz, seq, d), F32),
        scratch_shapes=scratch,
        compiler_params=_params(("arbitrary", "arbitrary")),
        name="hybrid_mixer",
    )(x, mod3, *small, w_in_b, *rest, w_out_b)


def _mlstm_kernel(x_ref, mod_ref, g_ref, wup_ref, cw_ref, cb_ref, wqk_ref, wv_ref, wg_ref, bg_ref,
                  ng_ref, skip_ref, wdown_ref,
                  o_ref,
                  ext, q_s, k_s, v_s, cst, cst_b, nst, mst):
    ln = x_ref.shape[1]
    w = MLSTM_WIDTH
    dh = MLSTM_HEADDIM
    nblk = w // MXU_DIM

    @pl.when(pl.program_id(1) == 0)
    def _():
        ext[0:CONV_PAD, :] = jnp.zeros((CONV_PAD, w), F32)
        cst[...] = jnp.zeros_like(cst)
        cst_b[...] = jnp.zeros_like(cst_b)
        nst[...] = jnp.zeros_like(nst)
        mst[...] = jnp.zeros_like(mst)

    x = x_ref[0]
    m = mod_ref[0]
    h = _modulate(x, g_ref[...], m[0:1], m[1:2]).astype(BF16)
    xm = _dot(h, wup_ref[:, 0:w])
    z = _dot(h, wup_ref[:, w:2 * w])
    ext[CONV_PAD:CONV_PAD + ln, :] = xm
    xc = _silu(_conv_rows(ext, cw_ref, cb_ref, 0, ln, slice(0, w)))
    ext[0:CONV_PAD, :] = xm[ln - CONV_PAD:ln, :]
    xc_b = xc.astype(BF16)
    xm_b = xm.astype(BF16)
    for b in range(nblk):
        sl = slice(MXU_DIM * b, MXU_DIM * (b + 1))
        qk = _dot(xc_b[:, sl], wqk_ref[b])
        q_s[:, sl] = qk[:, :MXU_DIM]
        k_s[:, sl] = qk[:, MXU_DIM:]
        v_s[:, sl] = _dot(xm_b[:, sl], wv_ref[b])
    gates = (_dot(q_s[...].astype(BF16), wg_ref[0:w, :]) + _dot(k_s[...].astype(BF16), wg_ref[w:2 * w, :])
             + _dot(v_s[...].astype(BF16), wg_ref[2 * w:3 * w, :]) + bg_ref[...])
    log_f = -_softplus(-gates)
    ri_ = lax.broadcasted_iota(jnp.int32, (ln, ln), 0)
    ci_ = lax.broadcasted_iota(jnp.int32, (ln, ln), 1)
    causal = ci_ <= ri_
    tril = jnp.where(causal, 1.0, 0.0).astype(BF16)
    bcum = _dot_lhs01(tril, log_f, 3)
    gates_t = gates.T
    bcum_t = bcum.T
    k_scale = dh ** -0.5
    heads = range(MLSTM_HEADS)
    hsl = [slice(dh * hd, dh * (hd + 1)) for hd in heads]
    st = []
    for hd in heads:
        fi = MLSTM_HEADS + hd
        bc_col = bcum[:, fi:fi + 1]
        bc_row = bcum_t[fi:fi + 1, :]
        i_col = gates[:, hd:hd + 1]
        i_row = gates_t[hd:hd + 1, :]
        m_prev = mst[hd][:, 0:1]
        log_d = jnp.where(causal, bc_col - bc_row + i_row, -jnp.inf)
        g_col = bc_col + m_prev
        m_rows = jnp.maximum(g_col, jnp.max(log_d, axis=-1, keepdims=True))
        b_last = bc_col[ln - 1:ln, :]
        log_w = b_last - bc_col + i_col
        m_next = jnp.maximum(b_last + m_prev, jnp.max(log_w, axis=0, keepdims=True))
        st.append(dict(m_rows=m_rows, w_inter=jnp.exp(g_col - m_rows), d_mat=jnp.exp(log_d - m_rows),
                       w_s=jnp.exp(log_w - m_next), decay=jnp.exp(b_last + m_prev - m_next), m_next=m_next))
    def stage2(hd):
        s = st[hd]
        q_h = q_s[:, hsl[hd]]
        k_h = k_s[:, hsl[hd]] * k_scale
        q_b = q_h.astype(BF16)
        v_b = v_s[:, hsl[hd]].astype(BF16)
        kw = k_h * s["w_s"]
        s.update(q_h=q_h, v_b=v_b, kw_sum=jnp.sum(kw, axis=0, keepdims=True),
                 qk=lax.dot_general(q_b, k_h.astype(BF16), (((1,), (1,)), ((), ())), preferred_element_type=F32),
                 qc=_dot(q_b, cst_b[hd]),
                 upd=lax.dot_general(kw.astype(BF16), v_b, (((0,), (0,)), ((), ())), preferred_element_type=F32))
    def stage3(hd):
        s = st[hd]
        s_mat = s["qk"] * s["d_mat"]
        n_prev = nst[hd]
        num = s["w_inter"] * s["qc"] + _dot(s_mat.astype(BF16), s["v_b"])
        den = (s["w_inter"] * jnp.sum(s["q_h"] * n_prev, axis=-1, keepdims=True)
               + jnp.sum(s_mat, axis=-1, keepdims=True))
        s["h_out"] = num / jnp.maximum(jnp.abs(den), jnp.exp(-s["m_rows"]))
        c_new = s["decay"] * cst[hd] + s["upd"]
        cst[hd] = c_new
        cst_b[hd] = c_new.astype(BF16)
        nst[hd] = s["decay"] * n_prev + s["kw_sum"]
        mst[hd] = jnp.broadcast_to(s["m_next"], (1, LANE))
    def stage4(hd):
        sl = hsl[hd]
        h_out = st[hd]["h_out"]
        mu = jnp.mean(h_out, axis=-1, keepdims=True)
        hc = h_out - mu
        var = jnp.mean(hc * hc, axis=-1, keepdims=True)
        hn = (hc * lax.rsqrt(var + EPS)) * ng_ref[:, sl]
        out = (hn + skip_ref[:, sl] * xc[:, sl]) * _silu(z[:, sl])
        return _dot(out.astype(BF16), wdown_ref[sl, :])

    y_mix = None
    for hd in heads:
        stage2(hd)
    for hd in heads:
        stage3(hd)
    for hd in heads:
        y_head = stage4(hd)
        y_mix = y_head if y_mix is None else y_mix + y_head

    o_ref[0] = x + (1.0 + m[2:3]) * y_mix


def _expand_block_diag(w, group):
    nb, bi, bo = w.shape
    rows = w.reshape(nb // group, group * bi, bo)
    r_idx = jnp.arange(group * bi)[:, None]
    c_idx = jnp.arange(group * bo)[None, :]
    period = (c_idx % bo == jnp.arange(bo)[:, None]).astype(w.dtype)
    tiled = jnp.einsum('gro,oc->grc', rows, period, precision=lax.Precision.HIGHEST)
    return jnp.where((r_idx // bi == c_idx // bo)[None], tiled, 0.0)


def _mlstm_call(x, mod3, g, w_up_b, w_down_b, o_idx, conv_w, conv_b, wq, wk, wv, w_gates, b_gates, norm_g, skip):
    bsz, seq, d = x.shape
    ln = MLSTM_CHUNK
    w = MLSTM_WIDTH
    grp = MXU_DIM // MLSTM_QKV_BLOCK
    wqk = jnp.concatenate([_expand_block_diag(wq, grp), _expand_block_diag(wk, grp)], axis=-1).astype(BF16)
    wvd = _expand_block_diag(wv, grp).astype(BF16)
    wg_p = jnp.zeros((3 * w, LANE), BF16).at[:, :2 * MLSTM_HEADS].set(w_gates.astype(BF16))
    bg_p = jnp.zeros((1, LANE), F32).at[0, :2 * MLSTM_HEADS].set(b_gates)
    row = lambda v: v.reshape(1, -1)
    mid = (conv_w, row(conv_b), wqk, wvd, wg_p, bg_p, row(norm_g), row(skip))
    in_specs = ([_tile_spec(ln, d), _mod_spec(d), _const_spec((1, d)), _select_spec(w_up_b.shape[1:], (o_idx,))]
                + [_const_spec(a.shape) for a in mid]
                + [_select_spec(w_down_b.shape[1:], (o_idx,))])
    scratch = [
        pltpu.VMEM((ln + CONV_PAD, w), F32),
        pltpu.VMEM((ln, w), F32),
        pltpu.VMEM((ln, w), F32),
        pltpu.VMEM((ln, w), F32),
        pltpu.VMEM((MLSTM_HEADS, MLSTM_HEADDIM, MLSTM_HEADDIM), F32),
        pltpu.VMEM((MLSTM_HEADS, MLSTM_HEADDIM, MLSTM_HEADDIM), BF16),
        pltpu.VMEM((MLSTM_HEADS, 1, MLSTM_HEADDIM), F32),
        pltpu.VMEM((MLSTM_HEADS, 1, LANE), F32),
    ]
    return pl.pallas_call(
        _mlstm_kernel,
        grid=(bsz, seq // ln),
        in_specs=in_specs,
        out_specs=_tile_spec(ln, d),
        out_shape=jax.ShapeDtypeStruct((bsz, seq, d), F32),
        scratch_shapes=scratch,
        compiler_params=_params(("arbitrary", "arbitrary")),
        name="mlstm_mixer",
    )(x, mod3, row(g), w_up_b, *mid, w_down_b)


def kernel(x, c, ada_w, ada_b, norm_g, ffn_w_gate, ffn_w_up, ffn_w_down, hyb_w_in, hyb_w_out, lru_conv_w, lru_conv_b, lru_wa, lru_ba, lru_wx, lru_bx, lru_lambda, ssd_conv_w, ssd_conv_b, ssd_dt_bias, ssd_a_log, ssd_d, ssd_norm_g, mlstm_w_up, mlstm_conv_w, mlstm_conv_b, mlstm_wq, mlstm_wk, mlstm_wv, mlstm_w_gates, mlstm_b_gates, mlstm_norm_g, mlstm_skip, mlstm_w_down, final_norm_g):
    depth = ada_w.shape[0]
    mod = _ada_call(c, ada_w, ada_b)
    wg_b, wu_b, wd_b = _cast_bf16(ffn_w_gate), _cast_bf16(ffn_w_up), _cast_bf16(ffn_w_down)
    hyb_in_b, hyb_out_b = _cast_bf16(hyb_w_in), _cast_bf16(hyb_w_out)
    up_b, down_b = _cast_bf16(mlstm_w_up), _cast_bf16(mlstm_w_down)
    for layer in range(depth):
        x = _ffn_call(x, mod[layer, :, 0], norm_g[layer, 0], wg_b, wu_b, wd_b, final_norm_g,
                      lead=(layer, 0), final_norm=False)
        if layer % 2 == 0:
            e = layer // 2
            x = _hyb_call(x, mod[layer, :, 1], norm_g[layer, 1], hyb_in_b, hyb_out_b, e,
                          hyb_w_in[e][:, HYB_SPLITS[-1]:], lru_conv_w[e], lru_conv_b[e], lru_wa[e], lru_ba[e],
                          lru_wx[e], lru_bx[e], lru_lambda[e], ssd_conv_w[e], ssd_conv_b[e], ssd_dt_bias[e],
                          ssd_a_log[e], ssd_d[e], ssd_norm_g[e])
        else:
            o = layer // 2
            x = _mlstm_call(x, mod[layer, :, 1], norm_g[layer, 1], up_b, down_b, o, mlstm_conv_w[o], mlstm_conv_b[o],
                            mlstm_wq[o], mlstm_wk[o], mlstm_wv[o], mlstm_w_gates[o], mlstm_b_gates[o],
                            mlstm_norm_g[o], mlstm_skip[o])
        x = _ffn_call(x, mod[layer, :, 2], norm_g[layer, 2], wg_b, wu_b, wd_b, final_norm_g,
                      lead=(layer, 1), final_norm=(layer == depth - 1))
    return x
```

```python
import functools
import math

import jax
import jax.numpy as jnp
from jax import lax
from jax.experimental import pallas as pl
from jax.experimental.pallas import tpu as pltpu

F32 = jnp.float32
BF16 = jnp.bfloat16

D_MODEL = 1024
N_SUB = 3
HALF_STEP = 0.5
D_FF = 2816
CONV_WIDTH = 4
EPS = 1e-6
LRU_WIDTH = D_MODEL
LRU_HEADS = 8
LRU_BLOCK = LRU_WIDTH // LRU_HEADS
LRU_C = 8.0
SSD_WIDTH = D_MODEL
SSD_HEADDIM = 64
SSD_HEADS = SSD_WIDTH // SSD_HEADDIM
SSD_GROUPS = 2
SSD_STATE = 128
SSD_CHUNK = 128
SSD_CONV_DIM = SSD_WIDTH + 2 * SSD_GROUPS * SSD_STATE
HYB_SPLITS = (LRU_WIDTH, 2 * LRU_WIDTH, 2 * LRU_WIDTH + SSD_WIDTH, 2 * LRU_WIDTH + SSD_WIDTH + SSD_CONV_DIM)
MLSTM_WIDTH = 2 * D_MODEL
MLSTM_HEADS = 4
MLSTM_HEADDIM = MLSTM_WIDTH // MLSTM_HEADS
MLSTM_QKV_BLOCK = 4

LANE = 128
SUBLANE = 8
PACKED_ROWS = 16
MXU_DIM = 256
VMEM_LIMIT_BYTES = 56 * 1024 * 1024

FFN_TILE = 1024
FFN_PART = 256
HYB_TILE = 512
MLSTM_CHUNK = 256
ADA_TILE_N = 3072
CAST_ROWS = 512
CONV_PAD = SUBLANE
ROW_STRIP = PACKED_ROWS
CONV_STRIP = 64
LRU_COLS = 2 * LRU_BLOCK
SSD_COLS = 512


def _silu(x):
    return x * jax.nn.sigmoid(x)


def _softplus(x):
    return jnp.maximum(x, 0.0) + jnp.log1p(jnp.exp(-jnp.abs(x)))


def _gelu_tanh(x):
    c = math.sqrt(2.0 / math.pi)
    return x * (0.5 * (1.0 + jnp.tanh(c * (x + 0.044715 * (x * x * x)))))


def _modulate(x, g, shift, scale):
    y = x * lax.rsqrt(jnp.mean(x * x, axis=-1, keepdims=True) + EPS)
    return (y * g) * (1.0 + scale) + shift


def _modulate_rows(x_ref, h_ref, g, shift, scale, t):
    for r in range(t // ROW_STRIP):
        rows = slice(ROW_STRIP * r, ROW_STRIP * (r + 1))
        h_ref[rows, :] = _modulate(x_ref[0, rows, :], g, shift, scale).astype(BF16)


def _split_bf16(x, n):
    parts = []
    r = x
    for _ in range(n):
        p = r.astype(BF16)
        parts.append(p)
        r = r - p.astype(F32)
    return parts


def _dot(a, b):
    return jnp.dot(a, b, preferred_element_type=F32)


def _dot_lhs01(m01, x, n):
    out = None
    for p in _split_bf16(x, n):
        t = _dot(m01, p)
        out = t if out is None else out + t
    return out


def _dot_rhs01(x, m01, n):
    out = None
    for p in _split_bf16(x, n):
        t = _dot(p, m01)
        out = t if out is None else out + t
    return out


def _conv_rows(ext_ref, w_ref, b_ref, r0, nrows, cols):
    ntile = nrows // SUBLANE
    tiles = [ext_ref[r0 + SUBLANE * i:r0 + SUBLANE * (i + 1), cols] for i in range(ntile + 1)]
    row = lax.broadcasted_iota(jnp.int32, tiles[0].shape, 0)
    bias = b_ref[:, cols]
    w_now = w_ref[CONV_WIDTH - 1:CONV_WIDTH, cols]
    acc = [bias + w_now * tiles[i + 1] for i in range(ntile)]
    for s in range(1, CONV_WIDTH):
        w_s = w_ref[CONV_WIDTH - 1 - s:CONV_WIDTH - s, cols]
        rot = [pltpu.roll(tl, s, 0) for tl in tiles]
        head = row < s
        acc = [acc[i] + w_s * jnp.where(head, rot[i], rot[i + 1]) for i in range(ntile)]
    return jnp.concatenate(acc, axis=0)


def _const_spec(shape):
    nd = len(shape)
    return pl.BlockSpec(shape, lambda *_: (0,) * nd, pipeline_mode=pl.Buffered(1))


def _select_spec(block, lead):
    nl = len(lead)
    return pl.BlockSpec((None,) * nl + tuple(block), lambda *_: tuple(lead) + (0,) * len(block),
                        pipeline_mode=pl.Buffered(1))


def _tile_spec(t, d):
    return pl.BlockSpec((1, t, d), lambda b, i: (b, i, 0))


def _mod_spec(d):
    return pl.BlockSpec((1, 3, d), lambda b, i: (b, 0, 0))


def _params(semantics):
    return pltpu.CompilerParams(dimension_semantics=semantics, vmem_limit_bytes=VMEM_LIMIT_BYTES)


def _cast_kernel(w_ref, o_ref):
    o_ref[...] = w_ref[...].astype(BF16)


def _cast_bf16(w):
    shape = w.shape
    cols = shape[-1]
    rows = math.prod(shape[:-1])
    out = pl.pallas_call(
        _cast_kernel,
        grid=(rows // CAST_ROWS,),
        in_specs=[pl.BlockSpec((CAST_ROWS, cols), lambda i: (i, 0))],
        out_specs=pl.BlockSpec((CAST_ROWS, cols), lambda i: (i, 0)),
        out_shape=jax.ShapeDtypeStruct((rows, cols), BF16),
        compiler_params=_params(("arbitrary",)),
        name="cast_bf16",
    )(w.reshape(rows, cols))
    return out.reshape(shape)


def _ada_kernel(c_ref, w_ref, b_ref, o_ref):
    c = c_ref[...]
    ca = _silu(c)
    w = w_ref[0]
    rows = c.shape[0]
    c1, c2 = _split_bf16(ca, 2)
    w1, w2 = _split_bf16(w, 2)
    hi = _dot(jnp.concatenate([c1, c2], axis=0), w1)
    o_ref[0] = hi[0:rows] + (_dot(c1, w2) + hi[rows:2 * rows]) + b_ref[0]


def _ada_call(c, ada_w, ada_b):
    depth, d, n = ada_w.shape
    bsz = c.shape[0]
    rows = -(-bsz // SUBLANE) * SUBLANE
    c_pad = jnp.zeros((rows, d), F32).at[:bsz].set(c)
    out = pl.pallas_call(
        _ada_kernel,
        grid=(depth, n // ADA_TILE_N),
        in_specs=[pl.BlockSpec((rows, d), lambda l, j: (0, 0)),
                  pl.BlockSpec((1, d, ADA_TILE_N), lambda l, j: (l, 0, j)),
                  pl.BlockSpec((1, 1, ADA_TILE_N), lambda l, j: (l, 0, j))],
        out_specs=pl.BlockSpec((1, rows, ADA_TILE_N), lambda l, j: (l, 0, j)),
        out_shape=jax.ShapeDtypeStruct((depth, rows, n), F32),
        compiler_params=_params(("arbitrary", "arbitrary")),
        name="ada_mod",
    )(c_pad, ada_w, ada_b.reshape(depth, 1, n))
    return out[:, :bsz].reshape(depth, bsz, N_SUB, 3, d)


def _ffn_kernel(x_ref, mod_ref, g_ref, wg_ref, wu_ref, wd_ref, fg_ref, o_ref, *, final_norm):
    m = mod_ref[0]
    t = x_ref.shape[1]
    n = t // FFN_PART
    parts = [slice(FFN_PART * p, FFN_PART * (p + 1)) for p in range(n)]

    def norm(p):
        return _modulate(x_ref[0, parts[p], :], g_ref[...], m[0:1], m[1:2]).astype(BF16)

    def project(h):
        return _dot(h, wg_ref[...]), _dot(h, wu_ref[...])

    h_next = norm(0)
    gu_next = project(h_next)
    h_next = norm(1) if n > 1 else None
    for p in range(n):
        gate, up = gu_next
        if p + 1 < n:
            gu_next = project(h_next)
            h_next = norm(p + 2) if p + 2 < n else None
        act = (_silu(gate) * up).astype(BF16)
        y = _dot(act, wd_ref[...])
        o = x_ref[0, parts[p], :] + (HALF_STEP * (1.0 + m[2:3])) * y
        if final_norm:
            o = (o * lax.rsqrt(jnp.mean(o * o, axis=-1, keepdims=True) + EPS)) * fg_ref[...]
        o_ref[0, parts[p], :] = o


def _ffn_call(x, mod3, g, wg, wu, wd, final_g, *, lead, final_norm):
    bsz, seq, d = x.shape
    f = wg.shape[-1]
    t = FFN_TILE
    return pl.pallas_call(
        functools.partial(_ffn_kernel, final_norm=final_norm),
        grid=(bsz, seq // t),
        in_specs=[_tile_spec(t, d), _mod_spec(d), _const_spec((1, d)),
                  _select_spec((d, f), lead), _select_spec((d, f), lead), _select_spec((f, d), lead),
                  _const_spec((1, d))],
        out_specs=_tile_spec(t, d),
        out_shape=jax.ShapeDtypeStruct((bsz, seq, d), F32),
        compiler_params=_params(("arbitrary", "arbitrary")),
        name="ffn",
    )(x, mod3, g.reshape(1, d), wg, wu, wd, final_g.reshape(1, d))


def _lru_group_scan(a, u):
    row = lax.broadcasted_iota(jnp.int32, a.shape, 0)
    for d in (1, 2, 4):
        a_prev = pltpu.roll(a, d, 0)
        u_prev = pltpu.roll(u, d, 0)
        keep = row >= d
        u = jnp.where(keep, a * u_prev + u, u)
        a = jnp.where(keep, a * a_prev, a)
    return a, u


def _ssd_chunk(c, causal, tril, low_half, xbc_s, dt_s, z_s, sstate, ycat, alog_ref, dexp_ref, sng_ref, e_ref):
    ln = SSD_CHUNK
    gn = SSD_GROUPS * SSD_STATE
    gw = SSD_WIDTH // SSD_GROUPS
    hpg = SSD_HEADS // SSD_GROUPS
    rows = slice(c * ln, (c + 1) * ln)
    xs = xbc_s[rows, 0:SSD_WIDTH]
    bm = xbc_s[rows, SSD_WIDTH:SSD_WIDTH + gn]
    cm = xbc_s[rows, SSD_WIDTH + gn:SSD_WIDTH + 2 * gn]
    dt = dt_s[rows, :]
    ac = dt * (-jnp.exp(alog_ref[...]))
    acs = _dot_lhs01(tril, ac, 3)
    acs_last = acs[ln - 1:ln, :]
    acs_t = acs.T
    dt_t = dt.T
    e = e_ref[...]
    dec_out = _dot_rhs01(jnp.exp(acs), e, 1)
    w_st = _dot_rhs01(dt * jnp.exp(acs_last - acs), e, 1)
    d_a = _dot_rhs01(jnp.broadcast_to(jnp.exp(acs_last), (SUBLANE, LANE)), e, 2)[0:1]
    xw_b = (xs * w_st).astype(BF16)
    s_prev = sstate[...]
    s_prev_b = s_prev.astype(BF16)
    ys = []
    sts = []
    for g in range(SSD_GROUPS):
        bg = bm[:, SSD_STATE * g:SSD_STATE * (g + 1)]
        cg_b = cm[:, SSD_STATE * g:SSD_STATE * (g + 1)].astype(BF16)
        bg_t = bg.T.astype(BF16)
        cb = _dot(cg_b, bg_t)
        y_off = _dot(cg_b, s_prev_b[:, gw * g:gw * (g + 1)])
        sts.append(_dot(bg_t, xw_b[:, gw * g:gw * (g + 1)]))
        for j in range(hpg // 2):
            pair = (hpg // 2) * g + j
            mats = []
            for hd in (2 * pair, 2 * pair + 1):
                diff = jnp.where(causal, acs[:, hd:hd + 1] - acs_t[hd:hd + 1, :], -jnp.inf)
                mats.append(cb * jnp.exp(diff) * dt_t[hd:hd + 1, :])
            lhs = jnp.concatenate(mats, axis=1).astype(BF16)
            xp = xs[:, LANE * pair:LANE * (pair + 1)]
            rhs = jnp.concatenate([jnp.where(low_half, xp, 0.0), jnp.where(low_half, 0.0, xp)],
                                  axis=0).astype(BF16)
            ys.append(_dot(lhs, rhs)
                      + y_off[:, LANE * j:LANE * (j + 1)] * dec_out[:, LANE * pair:LANE * (pair + 1)])
    sstate[...] = s_prev * d_a + jnp.concatenate(sts, axis=1)
    y = jnp.concatenate(ys, axis=1) + dexp_ref[...] * xs
    y = y * _silu(z_s[rows, :])
    for g in range(SSD_GROUPS):
        yg = y[:, gw * g:gw * (g + 1)]
        yn = (yg * lax.rsqrt(jnp.mean(yg * yg, axis=-1, keepdims=True) + EPS)) * sng_ref[:, gw * g:gw * (g + 1)]
        ycat[rows, LRU_WIDTH + gw * g:LRU_WIDTH + gw * (g + 1)] = yn.astype(BF16)


def _hyb_kernel(x_ref, mod_ref, g_ref, win_ref, wdt_ref, lcw_ref, lcb_ref, wai_ref, ba_ref, bx_ref, lam_ref,
                scw_ref, scb_ref, dtb_ref, alog_ref, dexp_ref, sng_ref, e_ref, wout_ref,
                o_ref,
                h_s, ext_l, xl_s, ri_s, gl_s, hcar, ext_s, xbc_s, dt_s, z_s, sstate, ycat):
    t = x_ref.shape[1]
    ln = SSD_CHUNK
    s0, s1, s2, s3 = HYB_SPLITS

    @pl.when(pl.program_id(1) == 0)
    def _():
        ext_l[0:CONV_PAD, :] = jnp.zeros((CONV_PAD, LRU_WIDTH), F32)
        ext_s[0:CONV_PAD, :] = jnp.zeros((CONV_PAD, SSD_CONV_DIM), F32)
        hcar[...] = jnp.zeros_like(hcar)
        sstate[...] = jnp.zeros_like(sstate)

    m = mod_ref[0]
    _modulate_rows(x_ref, h_s, g_ref[...], m[0:1], m[1:2], t)

    n_lru = LRU_WIDTH // LRU_COLS
    n_ssd = SSD_CONV_DIM // SSD_COLS

    def project_lru(p):
        cols = slice(LRU_COLS * p, LRU_COLS * (p + 1))
        ext_l[CONV_PAD:CONV_PAD + t, cols] = _dot(h_s[...], win_ref[:, s0 + LRU_COLS * p:s0 + LRU_COLS * (p + 1)])
        gl_s[:, cols] = _dot(h_s[...], win_ref[:, LRU_COLS * p:LRU_COLS * (p + 1)])

    def project_ssd(q):
        cols = slice(SSD_COLS * q, SSD_COLS * (q + 1))
        ext_s[CONV_PAD:CONV_PAD + t, cols] = _dot(h_s[...], win_ref[:, s2 + SSD_COLS * q:s2 + SSD_COLS * (q + 1)])

    project_lru(0)
    for p in range(n_lru):
        cols = slice(LRU_COLS * p, LRU_COLS * (p + 1))
        rcols = slice(2 * LRU_COLS * p, 2 * LRU_COLS * p + LRU_COLS)
        icols = slice(2 * LRU_COLS * p + LRU_COLS, 2 * LRU_COLS * (p + 1))
        if p + 1 < n_lru:
            project_lru(p + 1)
        else:
            project_ssd(0)
        for r in range(t // CONV_STRIP):
            xl_s[CONV_STRIP * r:CONV_STRIP * (r + 1), cols] = _conv_rows(ext_l, lcw_ref, lcb_ref, CONV_STRIP * r,
                                                                          CONV_STRIP, cols)
        ext_l[0:CONV_PAD, cols] = ext_l[t:t + CONV_PAD, cols]
        ri_s[:, 2 * LRU_COLS * p:2 * LRU_COLS * (p + 1)] = _dot(xl_s[:, cols].astype(BF16), wai_ref[p])
        sp = LRU_C * _softplus(-lam_ref[:, cols])
        ba = ba_ref[:, cols]
        bx = bx_ref[:, cols]
        carry = hcar[:, cols]
        for r in range(t // ROW_STRIP):
            rows = slice(ROW_STRIP * r, ROW_STRIP * (r + 1))
            rg = jax.nn.sigmoid(ri_s[rows, rcols] + ba)
            ig = jax.nn.sigmoid(ri_s[rows, icols] + bx)
            nlog_a = rg * sp
            a = jnp.exp(-nlog_a)
            u = jnp.sqrt(jnp.tanh(nlog_a) * (a * a + 1.0)) * (ig * xl_s[rows, cols])
            hs = []
            for q in range(ROW_STRIP // SUBLANE):
                a8, u8 = _lru_group_scan(a[SUBLANE * q:SUBLANE * (q + 1)], u[SUBLANE * q:SUBLANE * (q + 1)])
                h8 = a8 * carry + u8
                carry = h8[SUBLANE - 1:SUBLANE, :]
                hs.append(h8)
            ycat[rows, cols] = (jnp.concatenate(hs, axis=0) * _gelu_tanh(gl_s[rows, cols])).astype(BF16)
        hcar[:, cols] = carry

    for q in range(n_ssd):
        cols = slice(SSD_COLS * q, SSD_COLS * (q + 1))
        if q + 1 < n_ssd:
            project_ssd(q + 1)
        else:
            z_s[...] = _dot(h_s[...], win_ref[:, s1:s2])
            dt_s[...] = _softplus(_dot(h_s[...], wdt_ref[...]) + dtb_ref[...])
        for r in range(t // CONV_STRIP):
            xbc_s[CONV_STRIP * r:CONV_STRIP * (r + 1), cols] = _silu(
                _conv_rows(ext_s, scw_ref, scb_ref, CONV_STRIP * r, CONV_STRIP, cols))
        ext_s[0:CONV_PAD, cols] = ext_s[t:t + CONV_PAD, cols]
    y_lru = _dot(ycat[:, 0:LRU_WIDTH], wout_ref[0:LRU_WIDTH, :])

    ri_ = lax.broadcasted_iota(jnp.int32, (ln, ln), 0)
    ci_ = lax.broadcasted_iota(jnp.int32, (ln, ln), 1)
    causal = ci_ <= ri_
    tril = jnp.where(causal, 1.0, 0.0).astype(BF16)
    low_half = lax.broadcasted_iota(jnp.int32, (ln, LANE), 1) < SSD_HEADDIM
    for c in range(t // ln):
        _ssd_chunk(c, causal, tril, low_half, xbc_s, dt_s, z_s, sstate, ycat, alog_ref, dexp_ref, sng_ref, e_ref)

    y_mix = y_lru + _dot(ycat[:, LRU_WIDTH:], wout_ref[LRU_WIDTH:, :])
    o_ref[0] = x_ref[0] + (1.0 + m[2:3]) * y_mix


def _hyb_call(x, mod3, g, w_in_b, w_out_b, e_idx, w_dt, lru_conv_w, lru_conv_b, lru_wa, lru_ba, lru_wx, lru_bx,
              lru_lambda, ssd_conv_w, ssd_conv_b, ssd_dt_bias, ssd_a_log, ssd_d, ssd_norm_g):
    bsz, seq, d = x.shape
    t = HYB_TILE
    s3 = HYB_SPLITS[-1]
    w_dt_p = jnp.zeros((d, LANE), BF16).at[:, :SSD_HEADS].set(w_dt.astype(BF16))
    npair = LRU_HEADS // 2
    za = jnp.zeros((npair, LRU_BLOCK, LRU_BLOCK), F32)
    wa2 = lru_wa.reshape(npair, 2, LRU_BLOCK, LRU_BLOCK)
    wx2 = lru_wx.reshape(npair, 2, LRU_BLOCK, LRU_BLOCK)
    wai = jnp.concatenate([jnp.concatenate([wa2[:, 0], za, wx2[:, 0], za], axis=-1),
                           jnp.concatenate([za, wa2[:, 1], za, wx2[:, 1]], axis=-1)], axis=1).astype(BF16)
    pad_h = lambda v: jnp.zeros((1, LANE), F32).at[0, :SSD_HEADS].set(v)
    expand = jnp.zeros((LANE, SSD_WIDTH), F32).at[:SSD_HEADS].set(
        jnp.repeat(jnp.eye(SSD_HEADS, dtype=F32), SSD_HEADDIM, axis=1)).astype(BF16)
    row = lambda v: v.reshape(1, -1)
    small = (row(g),)
    rest = (w_dt_p, lru_conv_w, row(lru_conv_b), wai, row(lru_ba), row(lru_bx), row(lru_lambda),
            ssd_conv_w, row(ssd_conv_b), pad_h(ssd_dt_bias), pad_h(ssd_a_log),
            row(jnp.repeat(ssd_d, SSD_HEADDIM)), row(ssd_norm_g), expand)
    in_specs = ([_tile_spec(t, d), _mod_spec(d), _const_spec((1, d)), _select_spec((d, s3), (e_idx,))]
                + [_const_spec(a.shape) for a in rest]
                + [_select_spec(w_out_b.shape[1:], (e_idx,))])
    scratch = [
        pltpu.VMEM((t, d), BF16),
        pltpu.VMEM((t + CONV_PAD, LRU_WIDTH), F32),
        pltpu.VMEM((t, LRU_WIDTH), F32),
        pltpu.VMEM((t, 2 * LRU_WIDTH), F32),
        pltpu.VMEM((t, LRU_WIDTH), F32),
        pltpu.VMEM((1, LRU_WIDTH), F32),
        pltpu.VMEM((t + CONV_PAD, SSD_CONV_DIM), F32),
        pltpu.VMEM((t, SSD_CONV_DIM), F32),
        pltpu.VMEM((t, LANE), F32),
        pltpu.VMEM((t, SSD_WIDTH), F32),
        pltpu.VMEM((SSD_STATE, SSD_WIDTH), F32),
        pltpu.VMEM((t, LRU_WIDTH + SSD_WIDTH), BF16),
    ]
    return pl.pallas_call(
        _hyb_kernel,
        grid=(bsz, seq // t),
        in_specs=in_specs,
        out_specs=_tile_spec(t, d),
        out_shape=jax.ShapeDtypeStruct((bsz, seq, d), F32),
        scratch_shapes=scratch,
        compiler_params=_params(("arbitrary", "arbitrary")),
        name="hybrid_mixer",
    )(x, mod3, *small, w_in_b, *rest, w_out_b)


def _mlstm_kernel(x_ref, mod_ref, g_ref, wup_ref, cw_ref, cb_ref, wqk_ref, wv_ref, wg_ref, bg_ref,
                  ng_ref, skip_ref, wdown_ref,
                  o_ref,
                  ext, q_s, k_s, v_s, cst, cst_b, nst, mst):
    ln = x_ref.shape[1]
    w = MLSTM_WIDTH
    dh = MLSTM_HEADDIM
    nblk = w // MXU_DIM

    @pl.when(pl.program_id(1) == 0)
    def _():
        ext[0:CONV_PAD, :] = jnp.zeros((CONV_PAD, w), F32)
        cst[...] = jnp.zeros_like(cst)
        cst_b[...] = jnp.zeros_like(cst_b)
        nst[...] = jnp.zeros_like(nst)
        mst[...] = jnp.zeros_like(mst)

    x = x_ref[0]
    m = mod_ref[0]
    h = _modulate(x, g_ref[...], m[0:1], m[1:2]).astype(BF16)
    xm = _dot(h, wup_ref[:, 0:w])
    z = _dot(h, wup_ref[:, w:2 * w])
    ext[CONV_PAD:CONV_PAD + ln, :] = xm
    xc = _silu(_conv_rows(ext, cw_ref, cb_ref, 0, ln, slice(0, w)))
    ext[0:CONV_PAD, :] = xm[ln - CONV_PAD:ln, :]
    xc_b = xc.astype(BF16)
    xm_b = xm.astype(BF16)
    for b in range(nblk):
        sl = slice(MXU_DIM * b, MXU_DIM * (b + 1))
        qk = _dot(xc_b[:, sl], wqk_ref[b])
        q_s[:, sl] = qk[:, :MXU_DIM]
        k_s[:, sl] = qk[:, MXU_DIM:]
        v_s[:, sl] = _dot(xm_b[:, sl], wv_ref[b])
    gates = (_dot(q_s[...].astype(BF16), wg_ref[0:w, :]) + _dot(k_s[...].astype(BF16), wg_ref[w:2 * w, :])
             + _dot(v_s[...].astype(BF16), wg_ref[2 * w:3 * w, :]) + bg_ref[...])
    log_f = -_softplus(-gates)
    ri_ = lax.broadcasted_iota(jnp.int32, (ln, ln), 0)
    ci_ = lax.broadcasted_iota(jnp.int32, (ln, ln), 1)
    causal = ci_ <= ri_
    tril = jnp.where(causal, 1.0, 0.0).astype(BF16)
    bcum = _dot_lhs01(tril, log_f, 3)
    gates_t = gates.T
    bcum_t = bcum.T
    k_scale = dh ** -0.5
    heads = range(MLSTM_HEADS)
    hsl = [slice(dh * hd, dh * (hd + 1)) for hd in heads]
    st = []
    for hd in heads:
        fi = MLSTM_HEADS + hd
        bc_col = bcum[:, fi:fi + 1]
        bc_row = bcum_t[fi:fi + 1, :]
        i_col = gates[:, hd:hd + 1]
        i_row = gates_t[hd:hd + 1, :]
        m_prev = mst[hd][:, 0:1]
        log_d = jnp.where(causal, bc_col - bc_row + i_row, -jnp.inf)
        g_col = bc_col + m_prev
        m_rows = jnp.maximum(g_col, jnp.max(log_d, axis=-1, keepdims=True))
        b_last = bc_col[ln - 1:ln, :]
        log_w = b_last - bc_col + i_col
        m_next = jnp.maximum(b_last + m_prev, jnp.max(log_w, axis=0, keepdims=True))
        st.append(dict(m_rows=m_rows, w_inter=jnp.exp(g_col - m_rows), d_mat=jnp.exp(log_d - m_rows),
                       w_s=jnp.exp(log_w - m_next), decay=jnp.exp(b_last + m_prev - m_next), m_next=m_next))
    def stage2(hd):
        s = st[hd]
        q_h = q_s[:, hsl[hd]]
        k_h = k_s[:, hsl[hd]] * k_scale
        q_b = q_h.astype(BF16)
        v_b = v_s[:, hsl[hd]].astype(BF16)
        kw = k_h * s["w_s"]
        s.update(q_h=q_h, v_b=v_b, kw_sum=jnp.sum(kw, axis=0, keepdims=True),
                 qk=lax.dot_general(q_b, k_h.astype(BF16), (((1,), (1,)), ((), ())), preferred_element_type=F32),
                 qc=_dot(q_b, cst_b[hd]),
                 upd=lax.dot_general(kw.astype(BF16), v_b, (((0,), (0,)), ((), ())), preferred_element_type=F32))
    def stage3(hd):
        s = st[hd]
        s_mat = s["qk"] * s["d_mat"]
        n_prev = nst[hd]
        num = s["w_inter"] * s["qc"] + _dot(s_mat.astype(BF16), s["v_b"])
        den = (s["w_inter"] * jnp.sum(s["q_h"] * n_prev, axis=-1, keepdims=True)
               + jnp.sum(s_mat, axis=-1, keepdims=True))
        s["h_out"] = num / jnp.maximum(jnp.abs(den), jnp.exp(-s["m_rows"]))
        c_new = s["decay"] * cst[hd] + s["upd"]
        cst[hd] = c_new
        cst_b[hd] = c_new.astype(BF16)
        nst[hd] = s["decay"] * n_prev + s["kw_sum"]
        mst[hd] = jnp.broadcast_to(s["m_next"], (1, LANE))
    def stage4(hd):
        sl = hsl[hd]
        h_out = st[hd]["h_out"]
        mu = jnp.mean(h_out, axis=-1, keepdims=True)
        hc = h_out - mu
        var = jnp.mean(hc * hc, axis=-1, keepdims=True)
        hn = (hc * lax.rsqrt(var + EPS)) * ng_ref[:, sl]
        out = (hn + skip_ref[:, sl] * xc[:, sl]) * _silu(z[:, sl])
        return _dot(out.astype(BF16), wdown_ref[sl, :])

    y_mix = None
    for hd in heads:
        stage2(hd)
    for hd in heads:
        stage3(hd)
    for hd in heads:
        y_head = stage4(hd)
        y_mix = y_head if y_mix is None else y_mix + y_head

    o_ref[0] = x + (1.0 + m[2:3]) * y_mix


def _expand_block_diag(w, group):
    nb, bi, bo = w.shape
    rows = w.reshape(nb // group, group * bi, bo)
    r_idx = jnp.arange(group * bi)[:, None]
    c_idx = jnp.arange(group * bo)[None, :]
    period = (c_idx % bo == jnp.arange(bo)[:, None]).astype(w.dtype)
    tiled = jnp.einsum('gro,oc->grc', rows, period, precision=lax.Precision.HIGHEST)
    return jnp.where((r_idx // bi == c_idx // bo)[None], tiled, 0.0)


def _mlstm_call(x, mod3, g, w_up_b, w_down_b, o_idx, conv_w, conv_b, wq, wk, wv, w_gates, b_gates, norm_g, skip):
    bsz, seq, d = x.shape
    ln = MLSTM_CHUNK
    w = MLSTM_WIDTH
    grp = MXU_DIM // MLSTM_QKV_BLOCK
    wqk = jnp.concatenate([_expand_block_diag(wq, grp), _expand_block_diag(wk, grp)], axis=-1).astype(BF16)
    wvd = _expand_block_diag(wv, grp).astype(BF16)
    wg_p = jnp.zeros((3 * w, LANE), BF16).at[:, :2 * MLSTM_HEADS].set(w_gates.astype(BF16))
    bg_p = jnp.zeros((1, LANE), F32).at[0, :2 * MLSTM_HEADS].set(b_gates)
    row = lambda v: v.reshape(1, -1)
    mid = (conv_w, row(conv_b), wqk, wvd, wg_p, bg_p, row(norm_g), row(skip))
    in_specs = ([_tile_spec(ln, d), _mod_spec(d), _const_spec((1, d)), _select_spec(w_up_b.shape[1:], (o_idx,))]
                + [_const_spec(a.shape) for a in mid]
                + [_select_spec(w_down_b.shape[1:], (o_idx,))])
    scratch = [
        pltpu.VMEM((ln + CONV_PAD, w), F32),
        pltpu.VMEM((ln, w), F32),
        pltpu.VMEM((ln, w), F32),
        pltpu.VMEM((ln, w), F32),
        pltpu.VMEM((MLSTM_HEADS, MLSTM_HEADDIM, MLSTM_HEADDIM), F32),
        pltpu.VMEM((MLSTM_HEADS, MLSTM_HEADDIM, MLSTM_HEADDIM), BF16),
        pltpu.VMEM((MLSTM_HEADS, 1, MLSTM_HEADDIM), F32),
        pltpu.VMEM((MLSTM_HEADS, 1, LANE), F32),
    ]
    return pl.pallas_call(
        _mlstm_kernel,
        grid=(bsz, seq // ln),
        in_specs=in_specs,
        out_specs=_tile_spec(ln, d),
        out_shape=jax.ShapeDtypeStruct((bsz, seq, d), F32),
        scratch_shapes=scratch,
        compiler_params=_params(("arbitrary", "arbitrary")),
        name="mlstm_mixer",
    )(x, mod3, row(g), w_up_b, *mid, w_down_b)


def kernel(x, c, ada_w, ada_b, norm_g, ffn_w_gate, ffn_w_up, ffn_w_down, hyb_w_in, hyb_w_out, lru_conv_w, lru_conv_b, lru_wa, lru_ba, lru_wx, lru_bx, lru_lambda, ssd_conv_w, ssd_conv_b, ssd_dt_bias, ssd_a_log, ssd_d, ssd_norm_g, mlstm_w_up, mlstm_conv_w, mlstm_conv_b, mlstm_wq, mlstm_wk, mlstm_wv, mlstm_w_gates, mlstm_b_gates, mlstm_norm_g, mlstm_skip, mlstm_w_down, final_norm_g):
    depth = ada_w.shape[0]
    mod = _ada_call(c, ada_w, ada_b)
    wg_b, wu_b, wd_b = _cast_bf16(ffn_w_gate), _cast_bf16(ffn_w_up), _cast_bf16(ffn_w_down)
    hyb_in_b, hyb_out_b = _cast_bf16(hyb_w_in), _cast_bf16(hyb_w_out)
    up_b, down_b = _cast_bf16(mlstm_w_up), _cast_bf16(mlstm_w_down)
    for layer in range(depth):
        x = _ffn_call(x, mod[layer, :, 0], norm_g[layer, 0], wg_b, wu_b, wd_b, final_norm_g,
                      lead=(layer, 0), final_norm=False)
        if layer % 2 == 0:
            e = layer // 2
            x = _hyb_call(x, mod[layer, :, 1], norm_g[layer, 1], hyb_in_b, hyb_out_b, e,
                          hyb_w_in[e][:, HYB_SPLITS[-1]:], lru_conv_w[e], lru_conv_b[e], lru_wa[e], lru_ba[e],
                          lru_wx[e], lru_bx[e], lru_lambda[e], ssd_conv_w[e], ssd_conv_b[e], ssd_dt_bias[e],
                          ssd_a_log[e], ssd_d[e], ssd_norm_g[e])
        else:
            o = layer // 2
            x = _mlstm_call(x, mod[layer, :, 1], norm_g[layer, 1], up_b, down_b, o, mlstm_conv_w[o], mlstm_conv_b[o],
                            mlstm_wq[o], mlstm_wk[o], mlstm_wv[o], mlstm_w_gates[o], mlstm_b_gates[o],
                            mlstm_norm_g[o], mlstm_skip[o])
        x = _ffn_call(x, mod[layer, :, 2], norm_g[layer, 2], wg_b, wu_b, wd_b, final_norm_g,
                      lead=(layer, 1), final_norm=(layer == depth - 1))
    return x
```
